```python
import jax
import jax.numpy as jnp
from jax import lax
import numpy as np

D_MODEL = 2048
BATCH = 4
SEQ = 4096
DEPTH = 2


GRID_W = 64
CTX_LEN = 256
HEAD_DIM = 128
ATTN_SCALE = HEAD_DIM ** -0.5
NA_HEADS = 8
NA_KH = 8
NA_KW = 16
GQA_Q_HEADS = 8
GQA_KV_HEADS = 2
GQA_GROUP = GQA_Q_HEADS // GQA_KV_HEADS
ROPE_THETA = 10000.0
Q_BLOCK = 128
CHUNK = 128
SGU_GROUPS = 8
SGU_WIDTH = 2 * D_MODEL
N_EXPERTS = 16
EXPERT_FF = D_MODEL
CAPACITY_FACTOR = 2
N_MOD = 6
EPS = 1e-6
N_EVEN = (DEPTH + 1) // 2
N_ODD = DEPTH // 2
NA_W = NA_HEADS * HEAD_DIM
GQA_QW = GQA_Q_HEADS * HEAD_DIM
GQA_KVW = GQA_KV_HEADS * HEAD_DIM
Q_COLS = NA_W + GQA_QW
ATTN_IN = Q_COLS + 2 * NA_W + 2 * GQA_KVW
ATTN_OUT = NA_W + GQA_QW

kernel_name = "hybrid_natten_gqa_sgu_ecmoe_dit"


def rms_norm(x, g):
    xf = x.astype(jnp.float32)
    y = xf * lax.rsqrt(jnp.mean(xf * xf, axis=-1, keepdims=True) + EPS)
    return (y * g.astype(jnp.float32)).astype(x.dtype)


def layer_norm(x, g, b):
    xf = x.astype(jnp.float32)
    mu = jnp.mean(xf, axis=-1, keepdims=True)
    var = jnp.mean(jnp.square(xf - mu), axis=-1, keepdims=True)
    y = (xf - mu) * lax.rsqrt(var + EPS)
    return (y * g.astype(jnp.float32) + b.astype(jnp.float32)).astype(x.dtype)


def adaln(cvec, w, b):
    m = (jax.nn.silu(cvec) @ w + b)[..., None, :]
    return jnp.split(m, N_MOD, axis=-1)


def modulate(x, g, shift, scale):
    return rms_norm(x, g) * (1 + scale) + shift


def split_heads(z, n_heads):
    b, n, _ = z.shape
    return z.reshape(b, n, n_heads, HEAD_DIM).transpose(0, 2, 1, 3)


def merge_heads(z):
    b, h, n, dh = z.shape
    return z.transpose(0, 2, 1, 3).reshape(b, n, h * dh)


def axial_rope_tables(n, dtype):
    t = jnp.arange(n)
    row = (t // GRID_W).astype(jnp.float32)
    col = (t % GRID_W).astype(jnp.float32)
    axis_dims = HEAD_DIM // 2
    inv_freq = ROPE_THETA ** (-jnp.arange(0, axis_dims, 2, dtype=jnp.float32) / axis_dims)
    ang_r = row[:, None] * inv_freq
    ang_c = col[:, None] * inv_freq
    return tuple(a.astype(dtype) for a in (jnp.cos(ang_r), jnp.sin(ang_r), jnp.cos(ang_c), jnp.sin(ang_c)))


def rotate(x, cos, sin):
    x1, x2 = jnp.split(x, 2, axis=-1)
    return jnp.concatenate([x1 * cos - x2 * sin, x1 * sin + x2 * cos], axis=-1)


def apply_axial_rope(x, rope):
    cos_r, sin_r, cos_c, sin_c = rope
    xr, xc = jnp.split(x, 2, axis=-1)
    return jnp.concatenate([rotate(xr, cos_r, sin_r), rotate(xc, cos_c, sin_c)], axis=-1)


def gqa_block_attention(q, k, v):
    b, hk, g, n, dh = q.shape
    nb = n // Q_BLOCK
    qb = jnp.moveaxis(q.reshape(b, hk, g, nb, Q_BLOCK, dh), 3, 0)

    def block(qi):
        s = jnp.einsum('bkgqd,bkmd->bkgqm', qi, k).astype(jnp.float32) * ATTN_SCALE
        p = jax.nn.softmax(s, axis=-1).astype(v.dtype)
        return jnp.einsum('bkgqm,bkmd->bkgqd', p, v)

    o = lax.map(block, qb)
    return jnp.moveaxis(o, 0, 3).reshape(b, hk, g, n, dh)


def neighbourhood_attention(q, k, v, kc, vc, rpb):
    b, h, n, dh = q.shape
    rows = n // GRID_W
    kh = min(NA_KH, rows)
    kw = NA_KW
    k5 = k.reshape(b, h, rows, GRID_W, dh)
    v5 = v.reshape(b, h, rows, GRID_W, dh)
    cols = jnp.arange(GRID_W)
    col_start = jnp.clip(cols - kw // 2, 0, GRID_W - kw)
    col_idx = col_start[:, None] + jnp.arange(kw)
    dcol = col_idx - cols[:, None] + (NA_KW - 1)
    q_rows = jnp.moveaxis(q.reshape(b, h, rows, GRID_W, dh), 2, 0)

    def row_block(args):
        r, qr = args
        rs = jnp.clip(r - kh // 2, 0, rows - kh)
        kb = lax.dynamic_slice_in_dim(k5, rs, kh, axis=2)
        vb = lax.dynamic_slice_in_dim(v5, rs, kh, axis=2)
        kg = kb[:, :, :, col_idx, :]
        vg = vb[:, :, :, col_idx, :]
        drow = rs + jnp.arange(kh) - r + (NA_KH - 1)
        bias = jnp.transpose(rpb[:, drow][:, :, dcol], (0, 2, 1, 3))
        s_loc = jnp.einsum('bhwd,bhiwjd->bhwij', qr, kg) * ATTN_SCALE + bias
        s_ctx = jnp.einsum('bhwd,bhld->bhwl', qr, kc) * ATTN_SCALE
        s = jnp.concatenate([s_loc.reshape(b, h, GRID_W, kh * kw), s_ctx], axis=-1).astype(jnp.float32)
        p = jax.nn.softmax(s, axis=-1).astype(v.dtype)
        p_loc = p[..., :kh * kw].reshape(b, h, GRID_W, kh, kw)
        p_ctx = p[..., kh * kw:]
        return jnp.einsum('bhwij,bhiwjd->bhwd', p_loc, vg) + jnp.einsum('bhwl,bhld->bhwd', p_ctx, vc)

    out = lax.map(row_block, (jnp.arange(rows), q_rows))
    return jnp.moveaxis(out, 0, 2).reshape(b, h, n, dh)


def attention_mixer(a, ac, w_in, w_out, rpb, qn_g, kn_g, rope, need_ctx_out):
    b, n, _ = a.shape
    lc = ac.shape[1]
    qa, qb, ka, va, kb, vb = jnp.split(
        a @ w_in, [NA_W, Q_COLS, Q_COLS + NA_W, Q_COLS + 2 * NA_W, Q_COLS + 2 * NA_W + GQA_KVW], axis=-1)
    qa, ka, va = split_heads(qa, NA_HEADS), split_heads(ka, NA_HEADS), split_heads(va, NA_HEADS)
    qb = apply_axial_rope(rms_norm(split_heads(qb, GQA_Q_HEADS), qn_g), rope)
    kb = apply_axial_rope(rms_norm(split_heads(kb, GQA_KV_HEADS), kn_g), rope)
    vb = split_heads(vb, GQA_KV_HEADS)
    pc = ac @ (w_in if need_ctx_out else w_in[:, Q_COLS:])
    kvc = pc[..., Q_COLS:] if need_ctx_out else pc
    kac, vac, kbc, vbc = jnp.split(kvc, [NA_W, 2 * NA_W, 2 * NA_W + GQA_KVW], axis=-1)
    kac, vac = split_heads(kac, NA_HEADS), split_heads(vac, NA_HEADS)
    kbc = rms_norm(split_heads(kbc, GQA_KV_HEADS), kn_g)
    vbc = split_heads(vbc, GQA_KV_HEADS)
    oa = neighbourhood_attention(qa, ka, va, kac, vac, rpb)
    ob = gqa_block_attention(qb.reshape(b, GQA_KV_HEADS, GQA_GROUP, n, HEAD_DIM),
                             jnp.concatenate([kb, kbc], axis=2), jnp.concatenate([vb, vbc], axis=2))
    y = jnp.concatenate([merge_heads(oa), merge_heads(ob.reshape(b, GQA_Q_HEADS, n, HEAD_DIM))], axis=-1) @ w_out
    if not need_ctx_out:
        return y, None
    qac, qbc = jnp.split(pc[..., :Q_COLS], [NA_W], axis=-1)
    oac = gqa_block_attention(split_heads(qac, NA_HEADS)[:, :, None], kac, vac)[:, :, 0]
    qbc = rms_norm(split_heads(qbc, GQA_Q_HEADS), qn_g).reshape(b, GQA_KV_HEADS, GQA_GROUP, lc, HEAD_DIM)
    obc = gqa_block_attention(qbc, kbc, vbc).reshape(b, GQA_Q_HEADS, lc, HEAD_DIM)
    yc = jnp.concatenate([merge_heads(oac), merge_heads(obc)], axis=-1) @ w_out
    return y, yc


def sgu_mixer(a, w_in, w_out, ws, bs, ln_g, ln_b):
    b, n, _ = a.shape
    nc = n // CHUNK
    dg = SGU_WIDTH // SGU_GROUPS
    u, v = jnp.split(jax.nn.gelu(a @ w_in), 2, axis=-1)
    vg = layer_norm(v, ln_g, ln_b).reshape(b, nc, CHUNK, SGU_GROUPS, dg)
    mixed = jnp.einsum('gpq,bcqgd->bcpgd', ws, vg) + bs.T[None, None, :, :, None]
    return (u * mixed.reshape(b, n, SGU_WIDTH)) @ w_out


def expert_choice_moe(h, router, w1, w3, w2):
    b, n, d = h.shape
    cap = CAPACITY_FACTOR * n // N_EXPERTS
    aff = jax.nn.softmax((h @ router).astype(jnp.float32), axis=-1)
    gate, idx = lax.top_k(jnp.swapaxes(aff, 1, 2), cap)
    xg = jax.vmap(lambda hb, ib: hb[ib])(h, idx)
    hid = jax.nn.silu(jnp.einsum('becd,edf->becf', xg, w1)) * jnp.einsum('becd,edf->becf', xg, w3)
    y = jnp.einsum('becf,efd->becd', hid, w2) * gate[..., None].astype(h.dtype)
    return jax.vmap(lambda ib, yb: jnp.zeros((n, d), yb.dtype).at[ib.reshape(-1)].add(yb.reshape(-1, d)))(idx, y)


def setup_inputs(seed: int = 0) -> dict:
    key = jax.random.key(seed)
    keys = iter(jax.random.split(key, 24))

    def nrm(shape, std):
        return jax.random.normal(next(keys), shape, jnp.float32) * std

    def gain(shape):
        return 1.0 + nrm(shape, 0.02)

    D = D_MODEL
    return {
        "x": nrm((BATCH, SEQ, D), 1.0),
        "c": nrm((BATCH, D), 1.0),
        "ctx": nrm((BATCH, CTX_LEN, D), 1.0),
        "c_ctx": nrm((D,), 1.0),
        "mod_w": nrm((DEPTH, D, N_MOD * D), 0.5 * D ** -0.5),
        "mod_b": nrm((DEPTH, N_MOD * D), 0.02),
        "norm1_g": gain((DEPTH, D)),
        "norm2_g": gain((DEPTH, D)),
        "router": nrm((DEPTH, D, N_EXPERTS), D ** -0.5),
        "w1": nrm((DEPTH, N_EXPERTS, D, EXPERT_FF), D ** -0.5),
        "w3": nrm((DEPTH, N_EXPERTS, D, EXPERT_FF), D ** -0.5),
        "w2": nrm((DEPTH, N_EXPERTS, EXPERT_FF, D), EXPERT_FF ** -0.5),
        "attn_w_in": nrm((N_EVEN, D, ATTN_IN), D ** -0.5),
        "attn_w_out": nrm((N_EVEN, ATTN_OUT, D), ATTN_OUT ** -0.5),
        "na_rpb": nrm((N_EVEN, NA_HEADS, 2 * NA_KH - 1, 2 * NA_KW - 1), 0.1),
        "q_norm_g": gain((N_EVEN, HEAD_DIM)),
        "k_norm_g": gain((N_EVEN, HEAD_DIM)),
        "sgu_w_in": nrm((N_ODD, D, 2 * SGU_WIDTH), D ** -0.5),
        "sgu_w_out": nrm((N_ODD, SGU_WIDTH, D), SGU_WIDTH ** -0.5),
        "sgu_ws": nrm((N_ODD, SGU_GROUPS, CHUNK, CHUNK), CHUNK ** -0.5),
        "sgu_b": gain((N_ODD, SGU_GROUPS, CHUNK)),
        "sgu_ln_g": gain((N_ODD, SGU_WIDTH)),
        "sgu_ln_b": nrm((N_ODD, SGU_WIDTH), 0.02),
        "final_norm_g": gain((D,)),
    }


def reference(x, c, ctx, c_ctx, mod_w, mod_b, norm1_g, norm2_g, router, w1, w3, w2,
              attn_w_in, attn_w_out, na_rpb, q_norm_g, k_norm_g,
              sgu_w_in, sgu_w_out, sgu_ws, sgu_b, sgu_ln_g, sgu_ln_b, final_norm_g):
    rope = axial_rope_tables(x.shape[1], x.dtype)
    h_lat, h_ctx = x, ctx
    for l in range(DEPTH):
        ctx_needed_later = any(j % 2 == 0 for j in range(l + 1, DEPTH))
        ctx_used = (l % 2 == 0) or ctx_needed_later
        sh1, sc1, g1, sh2, sc2, g2 = adaln(c, mod_w[l], mod_b[l])
        a = modulate(h_lat, norm1_g[l], sh1, sc1)
        if ctx_used:
            csh1, csc1, cg1, csh2, csc2, cg2 = adaln(c_ctx, mod_w[l], mod_b[l])
            ac = modulate(h_ctx, norm1_g[l], csh1, csc1)
        if l % 2 == 0:
            e = l // 2
            y, yc = attention_mixer(a, ac, attn_w_in[e], attn_w_out[e], na_rpb[e],
                                    q_norm_g[e], k_norm_g[e], rope, ctx_needed_later)
        else:
            o = l // 2
            sgu_args = (sgu_w_in[o], sgu_w_out[o], sgu_ws[o], sgu_b[o], sgu_ln_g[o], sgu_ln_b[o])
            y = sgu_mixer(a, *sgu_args)
            yc = sgu_mixer(ac, *sgu_args) if ctx_needed_later else None
        h_lat = h_lat + g1 * y
        moe_args = (router[l], w1[l], w3[l], w2[l])
        h_lat = h_lat + g2 * expert_choice_moe(modulate(h_lat, norm2_g[l], sh2, sc2), *moe_args)
        if ctx_needed_later:
            h_ctx = h_ctx + cg1 * yc
            h_ctx = h_ctx + cg2 * expert_choice_moe(modulate(h_ctx, norm2_g[l], csh2, csc2), *moe_args)
    return rms_norm(h_lat, final_norm_g)
```

```python
import functools

import jax
import jax.numpy as jnp
from jax import lax
from jax.experimental import pallas as pl
from jax.experimental.pallas import tpu as pltpu

D_MODEL = 2048
DEPTH = 2
GRID_W = 64
CTX_LEN = 256
HEAD_DIM = 128
ATTN_SCALE = HEAD_DIM ** -0.5
NA_HEADS = 8
NA_KH = 8
NA_KW = 16
GQA_Q_HEADS = 8
GQA_KV_HEADS = 2
GQA_GROUP = GQA_Q_HEADS // GQA_KV_HEADS
ROPE_THETA = 10000.0
CHUNK = 128
SGU_GROUPS = 8
SGU_WIDTH = 2 * D_MODEL
N_EXPERTS = 16
CAPACITY_FACTOR = 2
N_MOD = 6
EPS = 1e-6
NA_W = NA_HEADS * HEAD_DIM
Q_COLS = NA_W + GQA_Q_HEADS * HEAD_DIM

QA_BLK, QB_BLK, KA_BLK, VA_BLK, KB_BLK, VB_BLK = 0, 8, 16, 24, 32, 34
CKA_BLK, CVA_BLK, CKB_BLK, CVB_BLK = 0, 8, 16, 18

NEG_INF = -1e30
V7X_VMEM_LIMIT_BYTES = 56 * 1024 * 1024
BF16 = jnp.bfloat16
F32 = jnp.float32

NA_Q_ROWS = 4
NA_K_ROWS = 12
NA_MASKED = 2 * NA_KH - 1


def _params(semantics):
    return pltpu.CompilerParams(dimension_semantics=semantics,
                                vmem_limit_bytes=V7X_VMEM_LIMIT_BYTES)


def _dot(a, b):
    return jnp.dot(a, b, preferred_element_type=F32)


def _dot_nt(a, b):
    return lax.dot_general(a, b, (((1,), (1,)), ((), ())), preferred_element_type=F32)


def _adaln_body(c_ref, w_ref, b_ref, o_ref):
    c = c_ref[...]
    s = (c * jax.nn.sigmoid(c)).astype(BF16)
    o_ref[0] = _dot(s, w_ref[0].astype(BF16)) + b_ref[0]


def adaln(cvec, mod_w, mod_b, tn=1024):
    L, d, n = mod_w.shape
    return pl.pallas_call(
        _adaln_body,
        grid=(L, n // tn),
        in_specs=[pl.BlockSpec((8, d), lambda l, j: (0, 0)),
                  pl.BlockSpec((1, d, tn), lambda l, j: (l, 0, j)),
                  pl.BlockSpec((1, 1, tn), lambda l, j: (l, 0, j))],
        out_specs=pl.BlockSpec((1, 8, tn), lambda l, j: (l, 0, j)),
        out_shape=jax.ShapeDtypeStruct((L, 8, n), F32),
        compiler_params=_params(("parallel", "arbitrary")),
        name="adaln",
    )(cvec, mod_w, mod_b.reshape(L, 1, n))


def _modulated(x, g, sh, sc):
    y = x * lax.rsqrt(jnp.mean(x * x, axis=-1, keepdims=True) + EPS) * g
    return y * (1 + sc) + sh


def _nmm_body(x_ref, g_ref, sh_ref, sc_ref, w_ref, o_ref, a_ref, *, gelu):
    @pl.when(pl.program_id(2) == 0)
    def _():
        a_ref[...] = _modulated(x_ref[0], g_ref[...], sh_ref[0], sc_ref[0]).astype(BF16)

    z = _dot(a_ref[...], w_ref[...])
    if gelu:
        z = jax.nn.gelu(z)
    o_ref[0] = z.astype(o_ref.dtype)


def norm_mod_matmul(h, g, shift, scale, w, *, tm, tn, gelu=False):
    b, n, d = h.shape
    n_out = w.shape[1]
    return pl.pallas_call(
        functools.partial(_nmm_body, gelu=gelu),
        grid=(b, n // tm, n_out // tn),
        in_specs=[pl.BlockSpec((1, tm, d), lambda bi, i, j: (bi, i, 0)),
                  pl.BlockSpec((1, d), lambda bi, i, j: (0, 0)),
                  pl.BlockSpec((1, 1, d), lambda bi, i, j: (bi, 0, 0)),
                  pl.BlockSpec((1, 1, d), lambda bi, i, j: (bi, 0, 0)),
                  pl.BlockSpec((d, tn), lambda bi, i, j: (0, j))],
        out_specs=pl.BlockSpec((1, tm, tn), lambda bi, i, j: (bi, i, j)),
        out_shape=jax.ShapeDtypeStruct((b, n, n_out), BF16),
        scratch_shapes=[pltpu.VMEM((tm, d), BF16)],
        compiler_params=_params(("parallel", "parallel", "arbitrary")),
        name="norm_mod_matmul",
    )(h, g, shift, scale, w)


def _mmr_body(a_ref, w_ref, h_ref, gate_ref, o_ref):
    o_ref[0] = h_ref[0] + gate_ref[0] * _dot(a_ref[0], w_ref[...])


def matmul_residual(a, w, h, gate, *, tm, tn):
    b, n, k = a.shape
    d = w.shape[1]
    return pl.pallas_call(
        _mmr_body,
        grid=(b, n // tm, d // tn),
        in_specs=[pl.BlockSpec((1, tm, k), lambda bi, i, j: (bi, i, 0)),
                  pl.BlockSpec((k, tn), lambda bi, i, j: (0, j)),
                  pl.BlockSpec((1, tm, tn), lambda bi, i, j: (bi, i, j)),
                  pl.BlockSpec((1, 1, tn), lambda bi, i, j: (bi, 0, j))],
        out_specs=pl.BlockSpec((1, tm, tn), lambda bi, i, j: (bi, i, j)),
        out_shape=jax.ShapeDtypeStruct((b, n, d), F32),
        compiler_params=_params(("parallel", "parallel", "arbitrary")),
        name="matmul_residual",
    )(a, w, h, gate)


def na_bias_tables(rpb):
    cols = jnp.arange(GRID_W)
    start = jnp.clip(cols - NA_KW // 2, 0, GRID_W - NA_KW)
    j = jnp.arange(GRID_W)
    inside = (j[None, :] >= start[:, None]) & (j[None, :] < start[:, None] + NA_KW)
    dcol = jnp.clip(j[None, :] - cols[:, None] + NA_KW - 1, 0, 2 * NA_KW - 2)
    t = jnp.where(inside[None, None], rpb[:, :, dcol], NEG_INF)
    t = jnp.concatenate([t, jnp.full_like(t[:, :1], NEG_INF)], axis=1)
    z = jnp.zeros_like(t)
    return jnp.concatenate([t, z], axis=-1), jnp.concatenate([z, t], axis=-1)


def _na_body(q_ref, k_ref, v_ref, kc_ref, vc_ref, tl_ref, tr_ref, o_ref, s_ref):
    rows = k_ref.shape[1] // GRID_W
    r0 = pl.program_id(2) * NA_Q_ROWS
    k0 = jnp.clip(r0 - NA_KH // 2, 0, rows - NA_K_ROWS)
    kstart = pl.multiple_of(k0 * GRID_W, GRID_W)
    q = q_ref[0]
    kw = k_ref[0, pl.ds(kstart, NA_K_ROWS * GRID_W), :]
    vw = v_ref[0, pl.ds(kstart, NA_K_ROWS * GRID_W), :]
    s_ref[...] = _dot_nt(q, kw) * ATTN_SCALE
    s_ctx = _dot_nt(q, kc_ref[0]) * ATTN_SCALE

    def table_index(kr, r, rs):
        valid = (kr >= rs) & (kr < rs + NA_KH)
        return jnp.where(valid, kr - r + NA_KH - 1, NA_MASKED)

    for qr in range(NA_Q_ROWS):
        r = r0 + qr
        rs = jnp.clip(r - NA_KH // 2, 0, rows - NA_KH)
        for p in range(NA_K_ROWS // 2):
            ia = table_index(k0 + 2 * p, r, rs)
            ib = table_index(k0 + 2 * p + 1, r, rs)
            blk = (slice(qr * GRID_W, (qr + 1) * GRID_W), slice(p * 2 * GRID_W, (p + 1) * 2 * GRID_W))
            s_ref[blk] = s_ref[blk] + tl_ref[0, ia] + tr_ref[0, ib]

    s_loc = s_ref[...]
    m = jnp.maximum(jnp.max(s_loc, axis=-1, keepdims=True), jnp.max(s_ctx, axis=-1, keepdims=True))
    p_loc = jnp.exp(s_loc - m)
    p_ctx = jnp.exp(s_ctx - m)
    denom = jnp.sum(p_loc, axis=-1, keepdims=True) + jnp.sum(p_ctx, axis=-1, keepdims=True)
    o = _dot(p_loc.astype(BF16), vw) + _dot(p_ctx.astype(BF16), vc_ref[0])
    o_ref[0] = (o / denom).astype(o_ref.dtype)


def neighbourhood_attention(proj, ctxp, tl, tr, attn_out_shape):
    b, n, _ = proj.shape
    lc = ctxp.shape[1]
    tq = NA_Q_ROWS * GRID_W
    hd = HEAD_DIM
    return pl.pallas_call(
        _na_body,
        grid=(b, NA_HEADS, n // tq),
        in_specs=[pl.BlockSpec((1, tq, hd), lambda bi, h, t: (bi, t, QA_BLK + h)),
                  pl.BlockSpec((1, n, hd), lambda bi, h, t: (bi, 0, KA_BLK + h)),
                  pl.BlockSpec((1, n, hd), lambda bi, h, t: (bi, 0, VA_BLK + h)),
                  pl.BlockSpec((1, lc, hd), lambda bi, h, t: (bi, 0, CKA_BLK + h)),
                  pl.BlockSpec((1, lc, hd), lambda bi, h, t: (bi, 0, CVA_BLK + h)),
                  pl.BlockSpec((1, 2 * NA_KH, GRID_W, 2 * GRID_W), lambda bi, h, t: (h, 0, 0, 0)),
                  pl.BlockSpec((1, 2 * NA_KH, GRID_W, 2 * GRID_W), lambda bi, h, t: (h, 0, 0, 0))],
        out_specs=pl.BlockSpec((1, tq, hd), lambda bi, h, t: (bi, t, h)),
        out_shape=jax.ShapeDtypeStruct(attn_out_shape, BF16),
        scratch_shapes=[pltpu.VMEM((tq, NA_K_ROWS * GRID_W), F32)],
        compiler_params=_params(("parallel", "parallel", "arbitrary")),
        name="neighbourhood_attention",
    )(proj, proj, proj, ctxp, ctxp, tl, tr)


def rope_tables(n):
    t = jnp.arange(n)
    row = (t // GRID_W).astype(F32)
    col = (t % GRID_W).astype(F32)
    axis_dims = HEAD_DIM // 2
    inv_freq = ROPE_THETA ** (-jnp.arange(0, axis_dims, 2, dtype=F32) / axis_dims)
    ang_r = row[:, None] * inv_freq
    ang_c = col[:, None] * inv_freq
    cos = jnp.concatenate([jnp.cos(ang_r), jnp.cos(ang_r), jnp.cos(ang_c), jnp.cos(ang_c)], axis=-1)
    sin = jnp.concatenate([-jnp.sin(ang_r), jnp.sin(ang_r), -jnp.sin(ang_c), jnp.sin(ang_c)], axis=-1)
    return cos, sin


def _head_rms(x, g):
    return x * lax.rsqrt(jnp.mean(x * x, axis=-1, keepdims=True) + EPS) * g


def _rope(x, cos, sin):
    quarter = HEAD_DIM // 4
    lane = lax.broadcasted_iota(jnp.int32, x.shape, 1)
    partner = jnp.where(lane % (2 * quarter) < quarter,
                        pltpu.roll(x, HEAD_DIM - quarter, 1), pltpu.roll(x, quarter, 1))
    return x * cos + partner * sin


def _gqa_body(q_ref, k_ref, v_ref, kc_ref, vc_ref, qg_ref, kg_ref, cq_ref, sq_ref, ck_ref, sk_ref,
              heads_a_ref, o_ref, kn_ref, kcn_ref):
    del heads_a_ref
    @pl.when(pl.program_id(2) == 0)
    def _():
        kg = kg_ref[...]
        kn_ref[...] = _rope(_head_rms(k_ref[0].astype(F32), kg), ck_ref[...], sk_ref[...]).astype(BF16)
        kcn_ref[...] = _head_rms(kc_ref[0].astype(F32), kg).astype(BF16)

    q = _rope(_head_rms(q_ref[0].astype(F32), qg_ref[...]), cq_ref[...], sq_ref[...])
    q = (q * ATTN_SCALE).astype(BF16)
    s_lat = _dot_nt(q, kn_ref[...])
    s_ctx = _dot_nt(q, kcn_ref[...])
    m = jnp.maximum(jnp.max(s_lat, axis=-1, keepdims=True), jnp.max(s_ctx, axis=-1, keepdims=True))
    p_lat = jnp.exp(s_lat - m)
    p_ctx = jnp.exp(s_ctx - m)
    denom = jnp.sum(p_lat, axis=-1, keepdims=True) + jnp.sum(p_ctx, axis=-1, keepdims=True)
    o = _dot(p_lat.astype(BF16), v_ref[0]) + _dot(p_ctx.astype(BF16), vc_ref[0])
    o_ref[0] = (o / denom).astype(o_ref.dtype)


def gqa_attention(proj, ctxp, qn_g, kn_g, cos, sin, attn_a, *, tq):
    b, n, _ = proj.shape
    lc = ctxp.shape[1]
    hd = HEAD_DIM
    nq = n // tq

    def qhead(kv, i):
        return kv * GQA_GROUP + i // nq

    return pl.pallas_call(
        _gqa_body,
        grid=(b, GQA_KV_HEADS, GQA_GROUP * nq),
        in_specs=[pl.BlockSpec((1, tq, hd), lambda bi, kv, i: (bi, i % nq, QB_BLK + qhead(kv, i))),
                  pl.BlockSpec((1, n, hd), lambda bi, kv, i: (bi, 0, KB_BLK + kv)),
                  pl.BlockSpec((1, n, hd), lambda bi, kv, i: (bi, 0, VB_BLK + kv)),
                  pl.BlockSpec((1, lc, hd), lambda bi, kv, i: (bi, 0, CKB_BLK + kv)),
                  pl.BlockSpec((1, lc, hd), lambda bi, kv, i: (bi, 0, CVB_BLK + kv)),
                  pl.BlockSpec((1, hd), lambda bi, kv, i: (0, 0)),
                  pl.BlockSpec((1, hd), lambda bi, kv, i: (0, 0)),
                  pl.BlockSpec((tq, hd), lambda bi, kv, i: (i % nq, 0)),
                  pl.BlockSpec((tq, hd), lambda bi, kv, i: (i % nq, 0)),
                  pl.BlockSpec((n, hd), lambda bi, kv, i: (0, 0)),
                  pl.BlockSpec((n, hd), lambda bi, kv, i: (0, 0)),
                  pl.BlockSpec(memory_space=pl.ANY)],
        out_specs=pl.BlockSpec((1, tq, hd), lambda bi, kv, i: (bi, i % nq, NA_HEADS + qhead(kv, i))),
        out_shape=jax.ShapeDtypeStruct(attn_a.shape, attn_a.dtype),
        scratch_shapes=[pltpu.VMEM((n, hd), BF16), pltpu.VMEM((lc, hd), BF16)],
        input_output_aliases={11: 0},
        compiler_params=_params(("parallel", "parallel", "arbitrary")),
        name="gqa_attention",
    )(proj, proj, proj, ctxp, ctxp, qn_g, kn_g, cos, sin, cos, sin, attn_a)


def _sgu_body(z_ref, ws_ref, bs_ref, g_ref, b_ref, o_ref):
    dg = SGU_WIDTH // SGU_GROUPS
    v = z_ref[0, :, SGU_WIDTH:].astype(F32)
    mu = jnp.mean(v, axis=-1, keepdims=True)
    vc = v - mu
    var = jnp.mean(vc * vc, axis=-1, keepdims=True)
    vn = (vc * lax.rsqrt(var + EPS) * g_ref[...] + b_ref[...]).astype(BF16)
    for g in range(SGU_GROUPS):
        cols = slice(g * dg, (g + 1) * dg)
        mixed = _dot(ws_ref[g].astype(BF16), vn[:, cols]) + bs_ref[:, g:g + 1]
        o_ref[0, :, cols] = (z_ref[0, :, cols].astype(F32) * mixed).astype(o_ref.dtype)


def sgu_mix(z, ws, bs, ln_g, ln_b):
    b, n, _ = z.shape
    return pl.pallas_call(
        _sgu_body,
        grid=(b, n // CHUNK),
        in_specs=[pl.BlockSpec((1, CHUNK, 2 * SGU_WIDTH), lambda bi, c: (bi, c, 0)),
                  pl.BlockSpec((SGU_GROUPS, CHUNK, CHUNK), lambda bi, c: (0, 0, 0)),
                  pl.BlockSpec((CHUNK, SGU_GROUPS), lambda bi, c: (0, 0)),
                  pl.BlockSpec((1, SGU_WIDTH), lambda bi, c: (0, 0)),
                  pl.BlockSpec((1, SGU_WIDTH), lambda bi, c: (0, 0))],
        out_specs=pl.BlockSpec((1, CHUNK, SGU_WIDTH), lambda bi, c: (bi, c, 0)),
        out_shape=jax.ShapeDtypeStruct((b, n, SGU_WIDTH), BF16),
        compiler_params=_params(("parallel", "arbitrary")),
        name="sgu_mix",
    )(z, ws, bs.T, ln_g.reshape(1, -1), ln_b.reshape(1, -1))


def _split_bf16(x):
    hi = x.astype(BF16)
    return hi, (x - hi.astype(F32)).astype(BF16)


def _router_body(x_ref, g_ref, sh_ref, sc_ref, rt_ref, a_ref, aff_ref):
    a = _modulated(x_ref[0], g_ref[...], sh_ref[0], sc_ref[0])
    a_ref[0] = a
    a_hi, a_lo = _split_bf16(a)
    r_hi, r_lo = _split_bf16(rt_ref[...])
    logits = _dot_nt(r_hi, a_hi) + (_dot_nt(r_hi, a_lo) + _dot_nt(r_lo, a_hi))
    e = jnp.exp(logits - jnp.max(logits, axis=0, keepdims=True))
    aff_ref[0] = e / jnp.sum(e, axis=0, keepdims=True)


def moe_router(h, g, shift, scale, router, *, tm):
    b, n, d = h.shape
    e = router.shape[1]
    return pl.pallas_call(
        _router_body,
        grid=(b, n // tm),
        in_specs=[pl.BlockSpec((1, tm, d), lambda bi, i: (bi, i, 0)),
                  pl.BlockSpec((1, d), lambda bi, i: (0, 0)),
                  pl.BlockSpec((1, 1, d), lambda bi, i: (bi, 0, 0)),
                  pl.BlockSpec((1, 1, d), lambda bi, i: (bi, 0, 0)),
                  pl.BlockSpec((e, d), lambda bi, i: (0, 0))],
        out_specs=[pl.BlockSpec((1, tm, d), lambda bi, i: (bi, i, 0)),
                   pl.BlockSpec((1, e, tm), lambda bi, i: (bi, 0, i))],
        out_shape=[jax.ShapeDtypeStruct((b, n, d), F32), jax.ShapeDtypeStruct((b, e, n), F32)],
        compiler_params=_params(("parallel", "arbitrary")),
        name="moe_router",
    )(h, g, shift, scale, router.T)


def _ffn_body(x_ref, w1_ref, w3_ref, w2_ref, gate_ref, o_ref, hid_ref, *, nt, tf):
    j = pl.program_id(1)

    @pl.when(j < nt)
    def _():
        x = x_ref[0]
        h1 = _dot(x, w1_ref[0].astype(BF16))
        h3 = _dot(x, w3_ref[0].astype(BF16))
        hid_ref[j] = (h1 * jax.nn.sigmoid(h1) * h3).astype(BF16)

    @pl.when(j >= nt)
    def _():
        acc = _dot(hid_ref[0], w2_ref[0, 0:tf, :].astype(BF16))
        for k in range(1, nt):
            acc = acc + _dot(hid_ref[k], w2_ref[0, k * tf:(k + 1) * tf, :].astype(BF16))
        o_ref[0] = acc * gate_ref[0]


def moe_ffn(xg, gate, w1, w3, w2, *, tf):
    e, m, d = xg.shape
    f = w1.shape[2]
    nt = f // tf
    nd = d // tf
    return pl.pallas_call(
        functools.partial(_ffn_body, nt=nt, tf=tf),
        grid=(e, nt + nd),
        in_specs=[pl.BlockSpec((1, m, d), lambda ei, j: (ei, 0, 0)),
                  pl.BlockSpec((1, d, tf), lambda ei, j: (ei, 0, jnp.minimum(j, nt - 1))),
                  pl.BlockSpec((1, d, tf), lambda ei, j: (ei, 0, jnp.minimum(j, nt - 1))),
                  pl.BlockSpec((1, f, tf), lambda ei, j: (ei, 0, jnp.maximum(j - nt, 0))),
                  pl.BlockSpec((1, m, 1), lambda ei, j: (ei, 0, 0))],
        out_specs=pl.BlockSpec((1, m, tf), lambda ei, j: (ei, 0, jnp.maximum(j - nt, 0))),
        out_shape=jax.ShapeDtypeStruct((e, m, d), F32),
        scratch_shapes=[pltpu.VMEM((nt, m, tf), BF16)],
        compiler_params=_params(("parallel", "arbitrary")),
        name="moe_ffn",
    )(xg, w1, w3, w2, gate)


def _res_body(h_ref, y_ref, gate_ref, fg_ref, o_ref, *, final):
    h = h_ref[0] + gate_ref[0] * y_ref[0]
    if final:
        h = h * lax.rsqrt(jnp.mean(h * h, axis=-1, keepdims=True) + EPS) * fg_ref[...]
    o_ref[0] = h


def gated_residual(h, y, gate, final_g, *, tm, final):
    b, n, d = h.shape
    return pl.pallas_call(
        functools.partial(_res_body, final=final),
        grid=(b, n // tm),
        in_specs=[pl.BlockSpec((1, tm, d), lambda bi, i: (bi, i, 0)),
                  pl.BlockSpec((1, tm, d), lambda bi, i: (bi, i, 0)),
                  pl.BlockSpec((1, 1, d), lambda bi, i: (bi, 0, 0)),
                  pl.BlockSpec((1, d), lambda bi, i: (0, 0))],
        out_specs=pl.BlockSpec((1, tm, d), lambda bi, i: (bi, i, 0)),
        out_shape=jax.ShapeDtypeStruct((b, n, d), F32),
        compiler_params=_params(("parallel", "arbitrary")),
        name="gated_residual",
    )(h, y, gate, final_g)


def _expert_choice_moe(h, g, shift, scale, router, w1, w3, w2):
    b, n, d = h.shape
    e = router.shape[1]
    cap = CAPACITY_FACTOR * n // e
    a, aff_t = moe_router(h, g, shift, scale, router, tm=512)
    gate, idx = lax.top_k(aff_t, cap)
    xg = jax.vmap(lambda ab, ib: ab[ib])(a, idx).astype(BF16)
    xg = xg.transpose(1, 0, 2, 3).reshape(e, b * cap, d)
    gate_e = gate.transpose(1, 0, 2).reshape(e, b * cap, 1)
    y = moe_ffn(xg, gate_e, w1, w3, w2, tf=256)
    y = y.reshape(e, b, cap, d).transpose(1, 0, 2, 3)
    return jax.vmap(lambda ib, yb: jnp.zeros((n, d), yb.dtype).at[ib.reshape(-1)].add(yb.reshape(-1, d)))(idx, y)


def kernel(x, c, ctx, c_ctx, mod_w, mod_b, norm1_g, norm2_g, router, w1, w3, w2, attn_w_in, attn_w_out,
           na_rpb, q_norm_g, k_norm_g, sgu_w_in, sgu_w_out, sgu_ws, sgu_b, sgu_ln_g, sgu_ln_b, final_norm_g):
    b, n, d = x.shape
    cvec = jnp.concatenate([c, c_ctx[None], jnp.zeros((8 - b - 1, d), c.dtype)], axis=0)
    mods = adaln(cvec, mod_w, mod_b)
    cos, sin = rope_tables(n)
    h = x
    for l in range(DEPTH):
        sh1, sc1, g1, sh2, sc2, g2 = (m[:b, None, :] for m in jnp.split(mods[l], N_MOD, axis=-1))
        n1 = norm1_g[l].reshape(1, d)
        if l % 2 == 0:
            ev = l // 2
            csh1, csc1 = (jnp.broadcast_to(m[b][None, None, :], (b, 1, d))
                          for m in jnp.split(mods[l], N_MOD, axis=-1)[:2])
            w_in = attn_w_in[ev].astype(BF16)
            proj = norm_mod_matmul(h, n1, sh1, sc1, w_in, tm=1024, tn=512)
            ctxp = norm_mod_matmul(ctx, n1, csh1, csc1, w_in[:, Q_COLS:], tm=CTX_LEN, tn=512)
            tl, tr = na_bias_tables(na_rpb[ev])
            attn = neighbourhood_attention(proj, ctxp, tl, tr, (b, n, 2 * NA_W))
            attn = gqa_attention(proj, ctxp, q_norm_g[ev].reshape(1, -1), k_norm_g[ev].reshape(1, -1),
                                 cos, sin, attn, tq=256)
            h = matmul_residual(attn, attn_w_out[ev].astype(BF16), h, g1, tm=1024, tn=512)
        else:
            o = l // 2
            z = norm_mod_matmul(h, n1, sh1, sc1, sgu_w_in[o].astype(BF16), tm=1024, tn=512, gelu=True)
            mixed = sgu_mix(z, sgu_ws[o], sgu_b[o], sgu_ln_g[o], sgu_ln_b[o])
            h = matmul_residual(mixed, sgu_w_out[o].astype(BF16), h, g1, tm=1024, tn=512)
        y = _expert_choice_moe(h, norm2_g[l].reshape(1, d), sh2, sc2, router[l], w1[l], w3[l], w2[l])
        h = gated_residual(h, y, g2, final_norm_g.reshape(1, d), tm=512, final=(l == DEPTH - 1))
    return h
```

```python
import functools

import jax
import jax.numpy as jnp
from jax import lax
from jax.experimental import pallas as pl
from jax.experimental.pallas import tpu as pltpu

D_MODEL = 2048
DEPTH = 2
GRID_W = 64
CTX_LEN = 256
HEAD_DIM = 128
ATTN_SCALE = HEAD_DIM ** -0.5
NA_HEADS = 8
NA_KH = 8
NA_KW = 16
GQA_Q_HEADS = 8
GQA_KV_HEADS = 2
GQA_GROUP = GQA_Q_HEADS // GQA_KV_HEADS
ROPE_THETA = 10000.0
CHUNK = 128
SGU_GROUPS = 8
SGU_WIDTH = 2 * D_MODEL
N_EXPERTS = 16
CAPACITY_FACTOR = 2
N_MOD = 6
EPS = 1e-6
NA_W = NA_HEADS * HEAD_DIM
Q_COLS = NA_W + GQA_Q_HEADS * HEAD_DIM

QA_BLK, QB_BLK, KA_BLK, VA_BLK, KB_BLK, VB_BLK = 0, 8, 16, 24, 32, 34
CKA_BLK, CVA_BLK, CKB_BLK, CVB_BLK = 0, 8, 16, 18

NEG_INF = -1e30
V7X_VMEM_LIMIT_BYTES = 56 * 1024 * 1024
BF16 = jnp.bfloat16
F32 = jnp.float32

NA_Q_ROWS = 4
NA_K_ROWS = 12
NA_MASKED = 2 * NA_KH - 1

LANES = 128
BF16_ROWS = 16
FFN_TILE = 256
GATHER_CAST_ROWS = 256
COMBINE_TOKENS = 256
COMBINE_WINDOW = 64


def _params(semantics):
    return pltpu.CompilerParams(dimension_semantics=semantics,
                                vmem_limit_bytes=V7X_VMEM_LIMIT_BYTES)


def _dot(a, b):
    return jnp.dot(a, b, preferred_element_type=F32)


def _dot_nt(a, b):
    return lax.dot_general(a, b, (((1,), (1,)), ((), ())), preferred_element_type=F32)


def _adaln_body(c_ref, w_ref, b_ref, o_ref):
    c = c_ref[...]
    s = (c * jax.nn.sigmoid(c)).astype(BF16)
    o_ref[0] = _dot(s, w_ref[0].astype(BF16)) + b_ref[0]


def adaln(cvec, mod_w, mod_b, tn=1024):
    L, d, n = mod_w.shape
    return pl.pallas_call(
        _adaln_body,
        grid=(L, n // tn),
        in_specs=[pl.BlockSpec((8, d), lambda l, j: (0, 0)),
                  pl.BlockSpec((1, d, tn), lambda l, j: (l, 0, j)),
                  pl.BlockSpec((1, 1, tn), lambda l, j: (l, 0, j))],
        out_specs=pl.BlockSpec((1, 8, tn), lambda l, j: (l, 0, j)),
        out_shape=jax.ShapeDtypeStruct((L, 8, n), F32),
        compiler_params=_params(("parallel", "arbitrary")),
        name="adaln",
    )(cvec, mod_w, mod_b.reshape(L, 1, n))


def _modulated(x, g, sh, sc):
    y = x * lax.rsqrt(jnp.mean(x * x, axis=-1, keepdims=True) + EPS) * g
    return y * (1 + sc) + sh


def _nmm_body(x_ref, g_ref, sh_ref, sc_ref, w_ref, o_ref, a_ref, *, gelu):
    @pl.when(pl.program_id(2) == 0)
    def _():
        a_ref[...] = _modulated(x_ref[0], g_ref[...], sh_ref[0], sc_ref[0]).astype(BF16)

    z = _dot(a_ref[...], w_ref[...])
    if gelu:
        z = jax.nn.gelu(z)
    o_ref[0] = z.astype(o_ref.dtype)


def norm_mod_matmul(h, g, shift, scale, w, *, tm, tn, gelu=False):
    b, n, d = h.shape
    n_out = w.shape[1]
    return pl.pallas_call(
        functools.partial(_nmm_body, gelu=gelu),
        grid=(b, n // tm, n_out // tn),
        in_specs=[pl.BlockSpec((1, tm, d), lambda bi, i, j: (bi, i, 0)),
                  pl.BlockSpec((1, d), lambda bi, i, j: (0, 0)),
                  pl.BlockSpec((1, 1, d), lambda bi, i, j: (bi, 0, 0)),
                  pl.BlockSpec((1, 1, d), lambda bi, i, j: (bi, 0, 0)),
                  pl.BlockSpec((d, tn), lambda bi, i, j: (0, j))],
        out_specs=pl.BlockSpec((1, tm, tn), lambda bi, i, j: (bi, i, j)),
        out_shape=jax.ShapeDtypeStruct((b, n, n_out), BF16),
        scratch_shapes=[pltpu.VMEM((tm, d), BF16)],
        compiler_params=_params(("parallel", "parallel", "arbitrary")),
        name="norm_mod_matmul",
    )(h, g, shift, scale, w)


def _mmr_body(*refs, n_parts):
    a_refs, w_refs = refs[:n_parts], refs[n_parts:2 * n_parts]
    h_ref, gate_ref, o_ref = refs[2 * n_parts:]
    acc = _dot(a_refs[0][0], w_refs[0][...])
    for a_ref, w_ref in zip(a_refs[1:], w_refs[1:]):
        acc = acc + _dot(a_ref[0], w_ref[...])
    o_ref[0] = h_ref[0] + gate_ref[0] * acc


def matmul_residual(parts, h, gate, *, tm, tn):
    b, n, d = h.shape
    a_list = [a for a, _ in parts]
    w_list = [w for _, w in parts]
    return pl.pallas_call(
        functools.partial(_mmr_body, n_parts=len(parts)),
        grid=(b, n // tm, d // tn),
        in_specs=([pl.BlockSpec((1, tm, a.shape[2]), lambda bi, i, j: (bi, i, 0)) for a in a_list]
                  + [pl.BlockSpec((w.shape[0], tn), lambda bi, i, j: (0, j)) for w in w_list]
                  + [pl.BlockSpec((1, tm, tn), lambda bi, i, j: (bi, i, j)),
                     pl.BlockSpec((1, 1, tn), lambda bi, i, j: (bi, 0, j))]),
        out_specs=pl.BlockSpec((1, tm, tn), lambda bi, i, j: (bi, i, j)),
        out_shape=jax.ShapeDtypeStruct((b, n, d), F32),
        compiler_params=_params(("parallel", "parallel", "arbitrary")),
        name="matmul_residual",
    )(*a_list, *w_list, h, gate)


def na_bias_tables(rpb):
    cols = jnp.arange(GRID_W)
    start = jnp.clip(cols - NA_KW // 2, 0, GRID_W - NA_KW)
    j = jnp.arange(GRID_W)
    inside = (j[None, :] >= start[:, None]) & (j[None, :] < start[:, None] + NA_KW)
    dcol = jnp.clip(j[None, :] - cols[:, None] + NA_KW - 1, 0, 2 * NA_KW - 2)
    t = jnp.where(inside[None, None], rpb[:, :, dcol], NEG_INF)
    t = jnp.concatenate([t, jnp.full_like(t[:, :1], NEG_INF)], axis=1)
    z = jnp.zeros_like(t)
    return jnp.concatenate([t, z], axis=-1), jnp.concatenate([z, t], axis=-1)


def _na_body(q_ref, k_ref, v_ref, kc_ref, vc_ref, tl_ref, tr_ref, o_ref, s_ref):
    rows = k_ref.shape[1] // GRID_W
    r0 = pl.program_id(2) * NA_Q_ROWS
    k0 = jnp.clip(r0 - NA_KH // 2, 0, rows - NA_K_ROWS)
    kstart = pl.multiple_of(k0 * GRID_W, GRID_W)
    q = q_ref[0]
    kw = k_ref[0, pl.ds(kstart, NA_K_ROWS * GRID_W), :]
    vw = v_ref[0, pl.ds(kstart, NA_K_ROWS * GRID_W), :]
    s_ref[...] = _dot_nt(q, kw) * ATTN_SCALE
    s_ctx = _dot_nt(q, kc_ref[0]) * ATTN_SCALE

    def table_index(kr, r, rs):
        valid = (kr >= rs) & (kr < rs + NA_KH)
        return jnp.where(valid, kr - r + NA_KH - 1, NA_MASKED)

    for qr in range(NA_Q_ROWS):
        r = r0 + qr
        rs = jnp.clip(r - NA_KH // 2, 0, rows - NA_KH)
        for p in range(NA_K_ROWS // 2):
            ia = table_index(k0 + 2 * p, r, rs)
            ib = table_index(k0 + 2 * p + 1, r, rs)
            blk = (slice(qr * GRID_W, (qr + 1) * GRID_W), slice(p * 2 * GRID_W, (p + 1) * 2 * GRID_W))
            s_ref[blk] = s_ref[blk] + tl_ref[0, ia] + tr_ref[0, ib]

    s_loc = s_ref[...]
    m = jnp.maximum(jnp.max(s_loc, axis=-1, keepdims=True), jnp.max(s_ctx, axis=-1, keepdims=True))
    p_loc = jnp.exp(s_loc - m)
    p_ctx = jnp.exp(s_ctx - m)
    denom = jnp.sum(p_loc, axis=-1, keepdims=True) + jnp.sum(p_ctx, axis=-1, keepdims=True)
    o = _dot(p_loc.astype(BF16), vw) + _dot(p_ctx.astype(BF16), vc_ref[0])
    o_ref[0] = (o / denom).astype(o_ref.dtype)


def neighbourhood_attention(proj, ctxp, tl, tr):
    b, n, _ = proj.shape
    lc = ctxp.shape[1]
    tq = NA_Q_ROWS * GRID_W
    hd = HEAD_DIM
    return pl.pallas_call(
        _na_body,
        grid=(b, NA_HEADS, n // tq),
        in_specs=[pl.BlockSpec((1, tq, hd), lambda bi, h, t: (bi, t, QA_BLK + h)),
                  pl.BlockSpec((1, n, hd), lambda bi, h, t: (bi, 0, KA_BLK + h)),
                  pl.BlockSpec((1, n, hd), lambda bi, h, t: (bi, 0, VA_BLK + h)),
                  pl.BlockSpec((1, lc, hd), lambda bi, h, t: (bi, 0, CKA_BLK + h)),
                  pl.BlockSpec((1, lc, hd), lambda bi, h, t: (bi, 0, CVA_BLK + h)),
                  pl.BlockSpec((1, 2 * NA_KH, GRID_W, 2 * GRID_W), lambda bi, h, t: (h, 0, 0, 0)),
                  pl.BlockSpec((1, 2 * NA_KH, GRID_W, 2 * GRID_W), lambda bi, h, t: (h, 0, 0, 0))],
        out_specs=pl.BlockSpec((1, tq, hd), lambda bi, h, t: (bi, t, h)),
        out_shape=jax.ShapeDtypeStruct((b, n, NA_W), BF16),
        scratch_shapes=[pltpu.VMEM((tq, NA_K_ROWS * GRID_W), F32)],
        compiler_params=_params(("parallel", "parallel", "arbitrary")),
        name="neighbourhood_attention",
    )(proj, proj, proj, ctxp, ctxp, tl, tr)


def rope_tables(n):
    t = jnp.arange(n)
    row = (t // GRID_W).astype(F32)
    col = (t % GRID_W).astype(F32)
    axis_dims = HEAD_DIM // 2
    inv_freq = ROPE_THETA ** (-jnp.arange(0, axis_dims, 2, dtype=F32) / axis_dims)
    ang_r = row[:, None] * inv_freq
    ang_c = col[:, None] * inv_freq
    cos = jnp.concatenate([jnp.cos(ang_r), jnp.cos(ang_r), jnp.cos(ang_c), jnp.cos(ang_c)], axis=-1)
    sin = jnp.concatenate([-jnp.sin(ang_r), jnp.sin(ang_r), -jnp.sin(ang_c), jnp.sin(ang_c)], axis=-1)
    return cos, sin


def _head_rms(x, g):
    return x * lax.rsqrt(jnp.mean(x * x, axis=-1, keepdims=True) + EPS) * g


def _rope(x, cos, sin):
    quarter = HEAD_DIM // 4
    lane = lax.broadcasted_iota(jnp.int32, x.shape, 1)
    partner = jnp.where(lane % (2 * quarter) < quarter,
                        pltpu.roll(x, HEAD_DIM - quarter, 1), pltpu.roll(x, quarter, 1))
    return x * cos + partner * sin


def _gqa_body(q_ref, k_ref, v_ref, kc_ref, vc_ref, qg_ref, kg_ref, cq_ref, sq_ref, ck_ref, sk_ref,
              o_ref, kn_ref, kcn_ref):
    @pl.when(pl.program_id(2) == 0)
    def _():
        kg = kg_ref[...]
        kn_ref[...] = _rope(_head_rms(k_ref[0].astype(F32), kg), ck_ref[...], sk_ref[...]).astype(BF16)
        kcn_ref[...] = _head_rms(kc_ref[0].astype(F32), kg).astype(BF16)

    q = _rope(_head_rms(q_ref[0].astype(F32), qg_ref[...]), cq_ref[...], sq_ref[...])
    q = (q * ATTN_SCALE).astype(BF16)
    s_lat = _dot_nt(q, kn_ref[...])
    s_ctx = _dot_nt(q, kcn_ref[...])
    m = jnp.maximum(jnp.max(s_lat, axis=-1, keepdims=True), jnp.max(s_ctx, axis=-1, keepdims=True))
    p_lat = jnp.exp(s_lat - m)
    p_ctx = jnp.exp(s_ctx - m)
    denom = jnp.sum(p_lat, axis=-1, keepdims=True) + jnp.sum(p_ctx, axis=-1, keepdims=True)
    o = _dot(p_lat.astype(BF16), v_ref[0]) + _dot(p_ctx.astype(BF16), vc_ref[0])
    o_ref[0] = (o / denom).astype(o_ref.dtype)


def gqa_attention(proj, ctxp, qn_g, kn_g, cos, sin, *, tq):
    b, n, _ = proj.shape
    lc = ctxp.shape[1]
    hd = HEAD_DIM
    nq = n // tq

    def qhead(kv, i):
        return kv * GQA_GROUP + i // nq

    return pl.pallas_call(
        _gqa_body,
        grid=(b, GQA_KV_HEADS, GQA_GROUP * nq),
        in_specs=[pl.BlockSpec((1, tq, hd), lambda bi, kv, i: (bi, i % nq, QB_BLK + qhead(kv, i))),
                  pl.BlockSpec((1, n, hd), lambda bi, kv, i: (bi, 0, KB_BLK + kv)),
                  pl.BlockSpec((1, n, hd), lambda bi, kv, i: (bi, 0, VB_BLK + kv)),
                  pl.BlockSpec((1, lc, hd), lambda bi, kv, i: (bi, 0, CKB_BLK + kv)),
                  pl.BlockSpec((1, lc, hd), lambda bi, kv, i: (bi, 0, CVB_BLK + kv)),
                  pl.BlockSpec((1, hd), lambda bi, kv, i: (0, 0)),
                  pl.BlockSpec((1, hd), lambda bi, kv, i: (0, 0)),
                  pl.BlockSpec((tq, hd), lambda bi, kv, i: (i % nq, 0)),
                  pl.BlockSpec((tq, hd), lambda bi, kv, i: (i % nq, 0)),
                  pl.BlockSpec((n, hd), lambda bi, kv, i: (0, 0)),
                  pl.BlockSpec((n, hd), lambda bi, kv, i: (0, 0))],
        out_specs=pl.BlockSpec((1, tq, hd), lambda bi, kv, i: (bi, i % nq, qhead(kv, i))),
        out_shape=jax.ShapeDtypeStruct((b, n, GQA_Q_HEADS * hd), BF16),
        scratch_shapes=[pltpu.VMEM((n, hd), BF16), pltpu.VMEM((lc, hd), BF16)],
        compiler_params=_params(("parallel", "parallel", "arbitrary")),
        name="gqa_attention",
    )(proj, proj, proj, ctxp, ctxp, qn_g, kn_g, cos, sin, cos, sin)


def _sgu_body(z_ref, ws_ref, bs_ref, g_ref, b_ref, o_ref):
    dg = SGU_WIDTH // SGU_GROUPS
    v = z_ref[0, :, SGU_WIDTH:].astype(F32)
    mu = jnp.mean(v, axis=-1, keepdims=True)
    vc = v - mu
    var = jnp.mean(vc * vc, axis=-1, keepdims=True)
    vn = (vc * lax.rsqrt(var + EPS) * g_ref[...] + b_ref[...]).astype(BF16)
    for g in range(SGU_GROUPS):
        cols = slice(g * dg, (g + 1) * dg)
        mixed = _dot(ws_ref[g].astype(BF16), vn[:, cols]) + bs_ref[:, g:g + 1]
        o_ref[0, :, cols] = (z_ref[0, :, cols].astype(F32) * mixed).astype(o_ref.dtype)


def sgu_mix(z, ws, bs, ln_g, ln_b):
    b, n, _ = z.shape
    return pl.pallas_call(
        _sgu_body,
        grid=(b, n // CHUNK),
        in_specs=[pl.BlockSpec((1, CHUNK, 2 * SGU_WIDTH), lambda bi, c: (bi, c, 0)),
                  pl.BlockSpec((SGU_GROUPS, CHUNK, CHUNK), lambda bi, c: (0, 0, 0)),
                  pl.BlockSpec((CHUNK, SGU_GROUPS), lambda bi, c: (0, 0)),
                  pl.BlockSpec((1, SGU_WIDTH), lambda bi, c: (0, 0)),
                  pl.BlockSpec((1, SGU_WIDTH), lambda bi, c: (0, 0))],
        out_specs=pl.BlockSpec((1, CHUNK, SGU_WIDTH), lambda bi, c: (bi, c, 0)),
        out_shape=jax.ShapeDtypeStruct((b, n, SGU_WIDTH), BF16),
        compiler_params=_params(("parallel", "arbitrary")),
        name="sgu_mix",
    )(z, ws, bs.T, ln_g.reshape(1, -1), ln_b.reshape(1, -1))


def _split_bf16(x):
    hi = x.astype(BF16)
    return hi, (x - hi.astype(F32)).astype(BF16)


def _router_body(x_ref, g_ref, sh_ref, sc_ref, rt_ref, a_ref, aff_ref):
    a = _modulated(x_ref[0], g_ref[...], sh_ref[0], sc_ref[0])
    a_ref[0] = a
    a_hi, a_lo = _split_bf16(a)
    r_hi, r_lo = _split_bf16(rt_ref[...])
    logits = _dot_nt(r_hi, a_hi) + (_dot_nt(r_hi, a_lo) + _dot_nt(r_lo, a_hi))
    e = jnp.exp(logits - jnp.max(logits, axis=0, keepdims=True))
    aff_ref[0] = e / jnp.sum(e, axis=0, keepdims=True)


def moe_router(h, g, shift, scale, router, *, tm):
    b, n, d = h.shape
    e = router.shape[1]
    return pl.pallas_call(
        _router_body,
        grid=(b, n // tm),
        in_specs=[pl.BlockSpec((1, tm, d), lambda bi, i: (bi, i, 0)),
                  pl.BlockSpec((1, d), lambda bi, i: (0, 0)),
                  pl.BlockSpec((1, 1, d), lambda bi, i: (bi, 0, 0)),
                  pl.BlockSpec((1, 1, d), lambda bi, i: (bi, 0, 0)),
                  pl.BlockSpec((e, d), lambda bi, i: (0, 0))],
        out_specs=[pl.BlockSpec((1, tm, d), lambda bi, i: (bi, i, 0)),
                   pl.BlockSpec((1, e, tm), lambda bi, i: (bi, 0, i))],
        out_shape=[jax.ShapeDtypeStruct((b, n, d), F32), jax.ShapeDtypeStruct((b, e, n), F32)],
        compiler_params=_params(("parallel", "arbitrary")),
        name="moe_router",
    )(h, g, shift, scale, router.T)


def _row_copy(a_hbm, xf_ref, sem, src_row, dst_row):
    return pltpu.make_async_copy(a_hbm.at[pl.ds(src_row, 1), :], xf_ref.at[pl.ds(dst_row, 1), :], sem.at[0])


def _ffn_body(rows_ref, a_hbm, w1_ref, w3_ref, w2_ref, gate_ref, o_ref, xf_ref, xb_ref, hid_ref, sem,
              *, nt, tf):
    j = pl.program_id(1)
    m = xf_ref.shape[0]

    @pl.when(j == 0)
    def _():
        def start(r, carry):
            _row_copy(a_hbm, xf_ref, sem, rows_ref[0, 0, r], r).start()
            return carry

        def wait(r, carry):
            _row_copy(a_hbm, xf_ref, sem, rows_ref[0, 0, r], r).wait()
            return carry

        lax.fori_loop(0, m, start, 0, unroll=8)
        lax.fori_loop(0, m, wait, 0, unroll=8)

        def cast(i, carry):
            sl = pl.ds(pl.multiple_of(i * GATHER_CAST_ROWS, GATHER_CAST_ROWS), GATHER_CAST_ROWS)
            xb_ref[sl, :] = xf_ref[sl, :].astype(BF16)
            return carry

        lax.fori_loop(0, m // GATHER_CAST_ROWS, cast, 0)

    @pl.when(j < nt)
    def _():
        x = xb_ref[...]
        h1 = _dot(x, w1_ref[0].astype(BF16))
        h3 = _dot(x, w3_ref[0].astype(BF16))
        hid_ref[j] = (h1 * jax.nn.sigmoid(h1) * h3).astype(BF16)

    @pl.when(j >= nt)
    def _():
        acc = _dot(hid_ref[0], w2_ref[0, 0:tf, :].astype(BF16))
        for k in range(1, nt):
            acc = acc + _dot(hid_ref[k], w2_ref[0, k * tf:(k + 1) * tf, :].astype(BF16))
        o_ref[0] = (acc * gate_ref[0]).astype(o_ref.dtype)


def moe_ffn(rows, a, gate, w1, w3, w2, *, tf):
    e, _, m = rows.shape
    d = a.shape[1]
    f = w1.shape[2]
    nt = f // tf
    nd = d // tf
    return pl.pallas_call(
        functools.partial(_ffn_body, nt=nt, tf=tf),
        grid=(e, nt + nd),
        in_specs=[pl.BlockSpec((1, 1, m), lambda ei, j: (ei, 0, 0), memory_space=pltpu.SMEM),
                  pl.BlockSpec(memory_space=pl.ANY),
                  pl.BlockSpec((1, d, tf), lambda ei, j: (ei, 0, jnp.minimum(j, nt - 1))),
                  pl.BlockSpec((1, d, tf), lambda ei, j: (ei, 0, jnp.minimum(j, nt - 1))),
                  pl.BlockSpec((1, f, tf), lambda ei, j: (ei, 0, jnp.maximum(j - nt, 0))),
                  pl.BlockSpec((1, m, 1), lambda ei, j: (ei, 0, 0))],
        out_specs=pl.BlockSpec((1, m, tf), lambda ei, j: (ei, 0, jnp.maximum(j - nt, 0))),
        out_shape=jax.ShapeDtypeStruct((e, m, d), BF16),
        scratch_shapes=[pltpu.VMEM((m, d), F32), pltpu.VMEM((m, d), BF16), pltpu.VMEM((nt, m, tf), BF16),
                        pltpu.SemaphoreType.DMA((1,))],
        compiler_params=_params(("arbitrary", "arbitrary")),
        name="moe_ffn",
    )(rows, a, w1, w3, w2, gate)


def prefix_constants(n):
    i = jnp.arange(LANES)
    within = (i[:, None] < i[None, :]).astype(BF16)
    tok_tile = jnp.arange(n) // LANES
    before = (tok_tile[:, None] < i[None, :]).astype(BF16)
    return within, before


def _exclusive_prefix(mask, within_ref, before_ref):
    n = mask.shape[1]
    m8 = jnp.broadcast_to(mask.astype(F32), (8, n)).astype(BF16)
    tile_start = _dot(m8, before_ref[...])
    parts = []
    for t in range(n // LANES):
        cols = slice(t * LANES, (t + 1) * LANES)
        parts.append(_dot(m8[:, cols], within_ref[...]) + tile_start[:, t:t + 1])
    return jnp.concatenate(parts, axis=1)[0:1], tile_start[0:1]


def _select_body(aff_ref, within_ref, before_ref, pos_ref, idx_ref, gate_ref, start_ref, *, cap):
    a = aff_ref[0, 0]
    n = a.shape[1]
    bits = pltpu.bitcast(a, jnp.int32)

    def refine(i, thr):
        cand = thr | lax.shift_left(jnp.int32(1), 30 - i)
        cnt = jnp.sum((bits >= cand).astype(F32), axis=-1, keepdims=True)
        return jnp.where(cnt >= cap, cand, thr)

    thr = lax.fori_loop(0, 31, refine, jnp.zeros((1, 1), jnp.int32))
    above = bits > thr
    tied = bits == thr
    need = cap - jnp.sum(above.astype(F32), axis=-1, keepdims=True)
    tied_rank, _ = _exclusive_prefix(tied, within_ref, before_ref)
    sel = above | (tied & (tied_rank < need))
    pos, tile_start = _exclusive_prefix(sel, within_ref, before_ref)
    pos = jnp.where(sel, pos, -1.0)
    pos_ref[0, 0] = pos
    start_ref[0, 0] = tile_start

    slot = lax.broadcasted_iota(jnp.int32, (cap, n), 0).astype(F32)
    onehot = jnp.where(jnp.broadcast_to(pos, (cap, n)) == slot, 1.0, 0.0).astype(BF16)
    tok = lax.broadcasted_iota(jnp.int32, (8, n), 1)
    row = lax.broadcasted_iota(jnp.int32, (8, n), 0)
    a8 = jnp.broadcast_to(a, (8, n))
    a_hi = a8.astype(BF16).astype(F32)
    a_mid = (a8 - a_hi).astype(BF16).astype(F32)
    a_lo = a8 - a_hi - a_mid
    table = jnp.where(row == 0, lax.shift_right_logical(tok, 6).astype(F32),
                      jnp.where(row == 1, (tok & 63).astype(F32),
                                jnp.where(row == 2, a_hi,
                                          jnp.where(row == 3, a_mid, jnp.where(row == 4, a_lo, 0.0)))))
    picked = _dot_nt(table.astype(BF16), onehot)
    idx_ref[0, 0] = (picked[0:1] * 64.0 + picked[1:2]).astype(jnp.int32)
    gate_ref[0, 0] = picked[2:3] + picked[3:4] + picked[4:5]


def moe_select(aff_t, cap):
    b, e, n = aff_t.shape
    within, before = prefix_constants(n)
    row = lambda last: pl.BlockSpec((1, 1, 1, last), lambda bi, ei: (bi, ei, 0, 0))
    return pl.pallas_call(
        functools.partial(_select_body, cap=cap),
        grid=(b, e),
        in_specs=[row(n),
                  pl.BlockSpec((LANES, LANES), lambda bi, ei: (0, 0)),
                  pl.BlockSpec((n, LANES), lambda bi, ei: (0, 0))],
        out_specs=[row(n), row(cap), row(cap), row(LANES)],
        out_shape=[jax.ShapeDtypeStruct((b, e, 1, n), F32), jax.ShapeDtypeStruct((b, e, 1, cap), jnp.int32),
                   jax.ShapeDtypeStruct((b, e, 1, cap), F32), jax.ShapeDtypeStruct((b, e, 1, LANES), F32)],
        compiler_params=_params(("parallel", "arbitrary")),
        name="moe_select",
    )(aff_t.reshape(b, e, 1, n), within, before)


def _combine_body(start_ref, pos_ref, y_ref, h_ref, gate_ref, fg_ref, o_ref, win_ref, *, tt, final):
    bi = pl.program_id(0)
    ti = pl.program_id(1)
    n_exp, _, cap, d = y_ref.shape
    w = COMBINE_WINDOW

    los = [start_ref[bi, e, ti] for e in range(n_exp)]
    his = [start_ref[bi, e, ti + 1] for e in range(n_exp)]
    bases = [(lo // BF16_ROWS) * BF16_ROWS for lo in los]
    n_pass = functools.reduce(jnp.maximum, [(hi - base + w - 1) // w for hi, base in zip(his, bases)])
    lane = lax.broadcasted_iota(jnp.int32, (tt, w), 1)

    def one_pass(p, acc):
        hots = []
        for e in range(n_exp):
            want = bases[e] + p * w
            st = pl.multiple_of(jnp.minimum(want, cap - w), BF16_ROWS)
            win_ref[e * w:(e + 1) * w, :] = y_ref[e, 0, pl.ds(st, w), :]
            slot = st + lane
            hit = (pos_ref[0, :, e:e + 1] == slot.astype(F32)) & (slot >= want)
            hots.append(jnp.where(hit, 1.0, 0.0).astype(BF16))
        return acc + _dot(jnp.concatenate(hots, axis=1), win_ref[...])

    acc = lax.fori_loop(0, n_pass, one_pass, jnp.zeros((tt, d), F32))
    h = h_ref[0] + gate_ref[0] * acc
    if final:
        h = h * lax.rsqrt(jnp.mean(h * h, axis=-1, keepdims=True) + EPS) * fg_ref[...]
    o_ref[0] = h


def moe_combine(tile_start, pos_t, y, h, gate, final_g, *, tt, final):
    b, n, d = h.shape
    e, _, cap, _ = y.shape
    grid_spec = pltpu.PrefetchScalarGridSpec(
        num_scalar_prefetch=1,
        grid=(b, n // tt),
        in_specs=[pl.BlockSpec((1, tt, e), lambda bi, i, s: (bi, i, 0)),
                  pl.BlockSpec((e, 1, cap, d), lambda bi, i, s: (0, bi, 0, 0), pipeline_mode=pl.Buffered(1)),
                  pl.BlockSpec((1, tt, d), lambda bi, i, s: (bi, i, 0)),
                  pl.BlockSpec((1, 1, d), lambda bi, i, s: (bi, 0, 0)),
                  pl.BlockSpec((1, d), lambda bi, i, s: (0, 0))],
        out_specs=pl.BlockSpec((1, tt, d), lambda bi, i, s: (bi, i, 0)),
        scratch_shapes=[pltpu.VMEM((e * COMBINE_WINDOW, d), BF16)])
    return pl.pallas_call(
        functools.partial(_combine_body, tt=tt, final=final),
        grid_spec=grid_spec,
        out_shape=jax.ShapeDtypeStruct((b, n, d), F32),
        compiler_params=_params(("arbitrary", "arbitrary")),
        name="moe_combine",
    )(tile_start, pos_t, y, h, gate, final_g)


def _expert_choice_moe(h, g, shift, scale, gate2, router, w1, w3, w2, final_g, *, final):
    b, n, d = h.shape
    e = router.shape[1]
    cap = CAPACITY_FACTOR * n // e
    a, aff_t = moe_router(h, g, shift, scale, router, tm=512)
    pos, idx, gate, start = moe_select(aff_t, cap)
    rows = idx[:, :, 0, :] + (jnp.arange(b, dtype=jnp.int32) * n)[:, None, None]
    rows = rows.transpose(1, 0, 2).reshape(e, 1, b * cap)
    gate_e = gate[:, :, 0, :].transpose(1, 0, 2).reshape(e, b * cap, 1)
    y = moe_ffn(rows, a.reshape(b * n, d), gate_e, w1, w3, w2, tf=FFN_TILE)
    stride = COMBINE_TOKENS // LANES
    tile_start = start[:, :, 0, 0:n // LANES + 1:stride].astype(jnp.int32)
    pos_t = pos[:, :, 0, :].transpose(0, 2, 1)
    return moe_combine(tile_start, pos_t, y.reshape(e, b, cap, d), h, gate2, final_g,
                       tt=COMBINE_TOKENS, final=final)


def kernel(x, c, ctx, c_ctx, mod_w, mod_b, norm1_g, norm2_g, router, w1, w3, w2, attn_w_in, attn_w_out,
           na_rpb, q_norm_g, k_norm_g, sgu_w_in, sgu_w_out, sgu_ws, sgu_b, sgu_ln_g, sgu_ln_b, final_norm_g):
    b, n, d = x.shape
    cvec = jnp.concatenate([c, c_ctx[None], jnp.zeros((8 - b - 1, d), c.dtype)], axis=0)
    mods = adaln(cvec, mod_w, mod_b)
    cos, sin = rope_tables(n)
    h = x
    for l in range(DEPTH):
        mod = [mods[l, :, k * d:(k + 1) * d] for k in range(N_MOD)]
        sh1, sc1, g1, sh2, sc2, g2 = (m[:b, None, :] for m in mod)
        n1 = norm1_g[l].reshape(1, d)
        if l % 2 == 0:
            ev = l // 2
            csh1, csc1 = (jnp.broadcast_to(m[b][None, None, :], (b, 1, d)) for m in mod[:2])
            w_in = attn_w_in[ev].astype(BF16)
            proj = norm_mod_matmul(h, n1, sh1, sc1, w_in, tm=1024, tn=512)
            ctxp = norm_mod_matmul(ctx, n1, csh1, csc1, w_in[:, Q_COLS:], tm=CTX_LEN, tn=512)
            tl, tr = na_bias_tables(na_rpb[ev])
            heads_a = neighbourhood_attention(proj, ctxp, tl, tr)
            heads_b = gqa_attention(proj, ctxp, q_norm_g[ev].reshape(1, -1), k_norm_g[ev].reshape(1, -1),
                                    cos, sin, tq=256)
            w_out = attn_w_out[ev].astype(BF16)
            h = matmul_residual([(heads_a, w_out[:NA_W]), (heads_b, w_out[NA_W:])], h, g1, tm=1024, tn=512)
        else:
            o = l // 2
            z = norm_mod_matmul(h, n1, sh1, sc1, sgu_w_in[o].astype(BF16), tm=1024, tn=512, gelu=True)
            mixed = sgu_mix(z, sgu_ws[o], sgu_b[o], sgu_ln_g[o], sgu_ln_b[o])
            h = matmul_residual([(mixed, sgu_w_out[o].astype(BF16))], h, g1, tm=1024, tn=512)
        h = _expert_choice_moe(h, norm2_g[l].reshape(1, d), sh2, sc2, g2, router[l], w1[l], w3[l], w2[l],
                               final_norm_g.reshape(1, d), final=(l == DEPTH - 1))
    return h
```

```python
import functools

import jax
import jax.numpy as jnp
from jax import lax
from jax.experimental import pallas as pl
from jax.experimental.pallas import tpu as pltpu

D_MODEL = 2048
DEPTH = 2
GRID_W = 64
CTX_LEN = 256
HEAD_DIM = 128
ATTN_SCALE = HEAD_DIM ** -0.5
NA_HEADS = 8
NA_KH = 8
NA_KW = 16
GQA_Q_HEADS = 8
GQA_KV_HEADS = 2
GQA_GROUP = GQA_Q_HEADS // GQA_KV_HEADS
ROPE_THETA = 10000.0
CHUNK = 128
SGU_GROUPS = 8
SGU_WIDTH = 2 * D_MODEL
N_EXPERTS = 16
CAPACITY_FACTOR = 2
N_MOD = 6
EPS = 1e-6
NA_W = NA_HEADS * HEAD_DIM
Q_COLS = NA_W + GQA_Q_HEADS * HEAD_DIM

QA_BLK, QB_BLK, KA_BLK, VA_BLK, KB_BLK, VB_BLK = 0, 8, 16, 24, 32, 34
CKA_BLK, CVA_BLK, CKB_BLK, CVB_BLK = 0, 8, 16, 18

NEG_INF = -1e30
V7X_VMEM_LIMIT_BYTES = 56 * 1024 * 1024
BF16 = jnp.bfloat16
F32 = jnp.float32

NA_Q_ROWS = 4
NA_K_ROWS = 12
NA_MASKED = 2 * NA_KH - 1

GQA_KEY_CHUNK = 512
GQA_Q_CHUNK = 256
LANES = 128
BF16_ROWS = 16
FFN_TILE = 256
GATHER_CAST_ROWS = 256
COMBINE_TOKENS = 256
COMBINE_WINDOW = 64


def _params(semantics):
    return pltpu.CompilerParams(dimension_semantics=semantics,
                                vmem_limit_bytes=V7X_VMEM_LIMIT_BYTES)


def _dot(a, b):
    return jnp.dot(a, b, preferred_element_type=F32)


def _dot_nt(a, b):
    return lax.dot_general(a, b, (((1,), (1,)), ((), ())), preferred_element_type=F32)


def _adaln_body(c_ref, w_ref, b_ref, o_ref):
    c = c_ref[...]
    s = (c * jax.nn.sigmoid(c)).astype(BF16)
    o_ref[0] = _dot(s, w_ref[0].astype(BF16)) + b_ref[0]


def adaln(cvec, mod_w, mod_b, tn=1024):
    L, d, n = mod_w.shape
    return pl.pallas_call(
        _adaln_body,
        grid=(L, n // tn),
        in_specs=[pl.BlockSpec((8, d), lambda l, j: (0, 0)),
                  pl.BlockSpec((1, d, tn), lambda l, j: (l, 0, j)),
                  pl.BlockSpec((1, 1, tn), lambda l, j: (l, 0, j))],
        out_specs=pl.BlockSpec((1, 8, tn), lambda l, j: (l, 0, j)),
        out_shape=jax.ShapeDtypeStruct((L, 8, n), F32),
        compiler_params=_params(("parallel", "arbitrary")),
        name="adaln",
    )(cvec, mod_w, mod_b.reshape(L, 1, n))


def _modulated(x, g, sh, sc):
    y = x * lax.rsqrt(jnp.mean(x * x, axis=-1, keepdims=True) + EPS) * g
    return y * (1 + sc) + sh


def _nmm_body(x_ref, g_ref, sh_ref, sc_ref, w_ref, o_ref, a_ref, *, gelu):
    @pl.when(pl.program_id(2) == 0)
    def _():
        a_ref[...] = _modulated(x_ref[0], g_ref[...], sh_ref[0], sc_ref[0]).astype(BF16)

    z = _dot(a_ref[...], w_ref[...])
    if gelu:
        z = jax.nn.gelu(z)
    o_ref[0] = z.astype(o_ref.dtype)


def norm_mod_matmul(h, g, shift, scale, w, *, tm, tn, gelu=False):
    b, n, d = h.shape
    n_out = w.shape[1]
    return pl.pallas_call(
        functools.partial(_nmm_body, gelu=gelu),
        grid=(b, n // tm, n_out // tn),
        in_specs=[pl.BlockSpec((1, tm, d), lambda bi, i, j: (bi, i, 0)),
                  pl.BlockSpec((1, d), lambda bi, i, j: (0, 0)),
                  pl.BlockSpec((1, 1, d), lambda bi, i, j: (bi, 0, 0)),
                  pl.BlockSpec((1, 1, d), lambda bi, i, j: (bi, 0, 0)),
                  pl.BlockSpec((d, tn), lambda bi, i, j: (0, j))],
        out_specs=pl.BlockSpec((1, tm, tn), lambda bi, i, j: (bi, i, j)),
        out_shape=jax.ShapeDtypeStruct((b, n, n_out), BF16),
        scratch_shapes=[pltpu.VMEM((tm, d), BF16)],
        compiler_params=_params(("parallel", "parallel", "arbitrary")),
        name="norm_mod_matmul",
    )(h, g, shift, scale, w)


def _mmr_body(*refs, n_parts):
    a_refs, w_refs = refs[:n_parts], refs[n_parts:2 * n_parts]
    h_ref, gate_ref, o_ref = refs[2 * n_parts:]
    acc = _dot(a_refs[0][0], w_refs[0][...])
    for a_ref, w_ref in zip(a_refs[1:], w_refs[1:]):
        acc = acc + _dot(a_ref[0], w_ref[...])
    o_ref[0] = h_ref[0] + gate_ref[0] * acc


def matmul_residual(parts, h, gate, *, tm, tn):
    b, n, d = h.shape
    a_list = [a for a, _ in parts]
    w_list = [w for _, w in parts]
    return pl.pallas_call(
        functools.partial(_mmr_body, n_parts=len(parts)),
        grid=(b, n // tm, d // tn),
        in_specs=([pl.BlockSpec((1, tm, a.shape[2]), lambda bi, i, j: (bi, i, 0)) for a in a_list]
                  + [pl.BlockSpec((w.shape[0], tn), lambda bi, i, j: (0, j)) for w in w_list]
                  + [pl.BlockSpec((1, tm, tn), lambda bi, i, j: (bi, i, j)),
                     pl.BlockSpec((1, 1, tn), lambda bi, i, j: (bi, 0, j))]),
        out_specs=pl.BlockSpec((1, tm, tn), lambda bi, i, j: (bi, i, j)),
        out_shape=jax.ShapeDtypeStruct((b, n, d), F32),
        compiler_params=_params(("parallel", "parallel", "arbitrary")),
        name="matmul_residual",
    )(*a_list, *w_list, h, gate)


def na_bias_tables(rpb):
    cols = jnp.arange(GRID_W)
    start = jnp.clip(cols - NA_KW // 2, 0, GRID_W - NA_KW)
    j = jnp.arange(GRID_W)
    inside = (j[None, :] >= start[:, None]) & (j[None, :] < start[:, None] + NA_KW)
    dcol = jnp.clip(j[None, :] - cols[:, None] + NA_KW - 1, 0, 2 * NA_KW - 2)
    t = jnp.where(inside[None, None], rpb[:, :, dcol], NEG_INF)
    t = jnp.concatenate([t, jnp.full_like(t[:, :1], NEG_INF)], axis=1)
    z = jnp.zeros_like(t)
    return jnp.concatenate([t, z], axis=-1), jnp.concatenate([z, t], axis=-1)


def _na_body(q_ref, k_ref, v_ref, kc_ref, vc_ref, tl_ref, tr_ref, o_ref, s_ref):
    rows = k_ref.shape[1] // GRID_W
    r0 = pl.program_id(2) * NA_Q_ROWS
    k0 = jnp.clip(r0 - NA_KH // 2, 0, rows - NA_K_ROWS)
    kstart = pl.multiple_of(k0 * GRID_W, GRID_W)
    q = q_ref[0]
    kw = k_ref[0, pl.ds(kstart, NA_K_ROWS * GRID_W), :]
    vw = v_ref[0, pl.ds(kstart, NA_K_ROWS * GRID_W), :]
    s_ref[...] = _dot_nt(q, kw) * ATTN_SCALE
    s_ctx = _dot_nt(q, kc_ref[0]) * ATTN_SCALE

    def table_index(kr, r, rs):
        valid = (kr >= rs) & (kr < rs + NA_KH)
        return jnp.where(valid, kr - r + NA_KH - 1, NA_MASKED)

    for qr in range(NA_Q_ROWS):
        r = r0 + qr
        rs = jnp.clip(r - NA_KH // 2, 0, rows - NA_KH)
        for p in range(NA_K_ROWS // 2):
            ia = table_index(k0 + 2 * p, r, rs)
            ib = table_index(k0 + 2 * p + 1, r, rs)
            blk = (slice(qr * GRID_W, (qr + 1) * GRID_W), slice(p * 2 * GRID_W, (p + 1) * 2 * GRID_W))
            s_ref[blk] = s_ref[blk] + tl_ref[0, ia] + tr_ref[0, ib]

    s_loc = s_ref[...]
    m = jnp.maximum(jnp.max(s_loc, axis=-1, keepdims=True), jnp.max(s_ctx, axis=-1, keepdims=True))
    p_loc = jnp.exp(s_loc - m)
    p_ctx = jnp.exp(s_ctx - m)
    denom = jnp.sum(p_loc, axis=-1, keepdims=True) + jnp.sum(p_ctx, axis=-1, keepdims=True)
    o = _dot(p_loc.astype(BF16), vw) + _dot(p_ctx.astype(BF16), vc_ref[0])
    o_ref[0] = (o / denom).astype(o_ref.dtype)


def neighbourhood_attention(proj, ctxp, tl, tr):
    b, n, _ = proj.shape
    lc = ctxp.shape[1]
    tq = NA_Q_ROWS * GRID_W
    hd = HEAD_DIM
    return pl.pallas_call(
        _na_body,
        grid=(b, NA_HEADS, n // tq),
        in_specs=[pl.BlockSpec((1, tq, hd), lambda bi, h, t: (bi, t, QA_BLK + h)),
                  pl.BlockSpec((1, n, hd), lambda bi, h, t: (bi, 0, KA_BLK + h)),
                  pl.BlockSpec((1, n, hd), lambda bi, h, t: (bi, 0, VA_BLK + h)),
                  pl.BlockSpec((1, lc, hd), lambda bi, h, t: (bi, 0, CKA_BLK + h)),
                  pl.BlockSpec((1, lc, hd), lambda bi, h, t: (bi, 0, CVA_BLK + h)),
                  pl.BlockSpec((1, 2 * NA_KH, GRID_W, 2 * GRID_W), lambda bi, h, t: (h, 0, 0, 0)),
                  pl.BlockSpec((1, 2 * NA_KH, GRID_W, 2 * GRID_W), lambda bi, h, t: (h, 0, 0, 0))],
        out_specs=pl.BlockSpec((1, tq, hd), lambda bi, h, t: (bi, t, h)),
        out_shape=jax.ShapeDtypeStruct((b, n, NA_W), BF16),
        scratch_shapes=[pltpu.VMEM((tq, NA_K_ROWS * GRID_W), F32)],
        compiler_params=_params(("parallel", "parallel", "arbitrary")),
        name="neighbourhood_attention",
    )(proj, proj, proj, ctxp, ctxp, tl, tr)


def rope_tables(n):
    t = jnp.arange(n)
    row = (t // GRID_W).astype(F32)
    col = (t % GRID_W).astype(F32)
    axis_dims = HEAD_DIM // 2
    inv_freq = ROPE_THETA ** (-jnp.arange(0, axis_dims, 2, dtype=F32) / axis_dims)
    ang_r = row[:, None] * inv_freq
    ang_c = col[:, None] * inv_freq
    cos = jnp.concatenate([jnp.cos(ang_r), jnp.cos(ang_r), jnp.cos(ang_c), jnp.cos(ang_c)], axis=-1)
    sin = jnp.concatenate([-jnp.sin(ang_r), jnp.sin(ang_r), -jnp.sin(ang_c), jnp.sin(ang_c)], axis=-1)
    return cos, sin


def _head_rms(x, g):
    return x * lax.rsqrt(jnp.mean(x * x, axis=-1, keepdims=True) + EPS) * g


def _rope(x, cos, sin):
    quarter = HEAD_DIM // 4
    lane = lax.broadcasted_iota(jnp.int32, x.shape, 1)
    partner = jnp.where(lane % (2 * quarter) < quarter,
                        pltpu.roll(x, HEAD_DIM - quarter, 1), pltpu.roll(x, quarter, 1))
    return x * cos + partner * sin


def _gqa_body(q_ref, k_ref, v_ref, kc_ref, vc_ref, qg_ref, kg_ref, cq_ref, sq_ref, ck_ref, sk_ref,
              o_ref, kt_ref):
    n = k_ref.shape[1]
    lc = kc_ref.shape[1]

    @pl.when(pl.program_id(2) == 0)
    def _():
        kg = kg_ref[...]
        for c in range(n // GQA_KEY_CHUNK):
            rows = slice(c * GQA_KEY_CHUNK, (c + 1) * GQA_KEY_CHUNK)
            kn = _rope(_head_rms(k_ref[0, rows, :].astype(F32), kg), ck_ref[rows, :], sk_ref[rows, :])
            kt_ref[:, rows] = kn.T.astype(BF16)
        kt_ref[:, n:n + lc] = _head_rms(kc_ref[0].astype(F32), kg).T.astype(BF16)

    for c in range(q_ref.shape[1] // GQA_Q_CHUNK):
        rows = slice(c * GQA_Q_CHUNK, (c + 1) * GQA_Q_CHUNK)
        q = _rope(_head_rms(q_ref[0, rows, :].astype(F32), qg_ref[...]), cq_ref[rows, :], sq_ref[rows, :])
        q = (q * ATTN_SCALE).astype(BF16)
        s = _dot(q, kt_ref[...])
        p = jnp.exp(s - jnp.max(s, axis=-1, keepdims=True))
        denom = jnp.sum(p, axis=-1, keepdims=True)
        pb = p.astype(BF16)
        o = _dot(pb[:, :n], v_ref[0]) + _dot(pb[:, n:], vc_ref[0])
        o_ref[0, rows, :] = (o / denom).astype(o_ref.dtype)


def gqa_attention(proj, ctxp, qn_g, kn_g, cos, sin, *, tq):
    b, n, _ = proj.shape
    lc = ctxp.shape[1]
    hd = HEAD_DIM
    nq = n // tq

    def qhead(kv, i):
        return kv * GQA_GROUP + i // nq

    return pl.pallas_call(
        _gqa_body,
        grid=(b, GQA_KV_HEADS, GQA_GROUP * nq),
        in_specs=[pl.BlockSpec((1, tq, hd), lambda bi, kv, i: (bi, i % nq, QB_BLK + qhead(kv, i))),
                  pl.BlockSpec((1, n, hd), lambda bi, kv, i: (bi, 0, KB_BLK + kv)),
                  pl.BlockSpec((1, n, hd), lambda bi, kv, i: (bi, 0, VB_BLK + kv)),
                  pl.BlockSpec((1, lc, hd), lambda bi, kv, i: (bi, 0, CKB_BLK + kv)),
                  pl.BlockSpec((1, lc, hd), lambda bi, kv, i: (bi, 0, CVB_BLK + kv)),
                  pl.BlockSpec((1, hd), lambda bi, kv, i: (0, 0)),
                  pl.BlockSpec((1, hd), lambda bi, kv, i: (0, 0)),
                  pl.BlockSpec((tq, hd), lambda bi, kv, i: (i % nq, 0)),
                  pl.BlockSpec((tq, hd), lambda bi, kv, i: (i % nq, 0)),
                  pl.BlockSpec((n, hd), lambda bi, kv, i: (0, 0)),
                  pl.BlockSpec((n, hd), lambda bi, kv, i: (0, 0))],
        out_specs=pl.BlockSpec((1, tq, hd), lambda bi, kv, i: (bi, i % nq, qhead(kv, i))),
        out_shape=jax.ShapeDtypeStruct((b, n, GQA_Q_HEADS * hd), BF16),
        scratch_shapes=[pltpu.VMEM((hd, n + lc), BF16)],
        compiler_params=_params(("parallel", "parallel", "arbitrary")),
        name="gqa_attention",
    )(proj, proj, proj, ctxp, ctxp, qn_g, kn_g, cos, sin, cos, sin)


def _sgu_body(z_ref, ws_ref, bs_ref, g_ref, b_ref, o_ref):
    dg = SGU_WIDTH // SGU_GROUPS
    v = z_ref[0, :, SGU_WIDTH:].astype(F32)
    mu = jnp.mean(v, axis=-1, keepdims=True)
    vc = v - mu
    var = jnp.mean(vc * vc, axis=-1, keepdims=True)
    vn = (vc * lax.rsqrt(var + EPS) * g_ref[...] + b_ref[...]).astype(BF16)
    for g in range(SGU_GROUPS):
        cols = slice(g * dg, (g + 1) * dg)
        mixed = _dot(ws_ref[g].astype(BF16), vn[:, cols]) + bs_ref[:, g:g + 1]
        o_ref[0, :, cols] = (z_ref[0, :, cols].astype(F32) * mixed).astype(o_ref.dtype)


def sgu_mix(z, ws, bs, ln_g, ln_b):
    b, n, _ = z.shape
    return pl.pallas_call(
        _sgu_body,
        grid=(b, n // CHUNK),
        in_specs=[pl.BlockSpec((1, CHUNK, 2 * SGU_WIDTH), lambda bi, c: (bi, c, 0)),
                  pl.BlockSpec((SGU_GROUPS, CHUNK, CHUNK), lambda bi, c: (0, 0, 0)),
                  pl.BlockSpec((CHUNK, SGU_GROUPS), lambda bi, c: (0, 0)),
                  pl.BlockSpec((1, SGU_WIDTH), lambda bi, c: (0, 0)),
                  pl.BlockSpec((1, SGU_WIDTH), lambda bi, c: (0, 0))],
        out_specs=pl.BlockSpec((1, CHUNK, SGU_WIDTH), lambda bi, c: (bi, c, 0)),
        out_shape=jax.ShapeDtypeStruct((b, n, SGU_WIDTH), BF16),
        compiler_params=_params(("parallel", "arbitrary")),
        name="sgu_mix",
    )(z, ws, bs.T, ln_g.reshape(1, -1), ln_b.reshape(1, -1))


def _split_bf16(x):
    hi = x.astype(BF16)
    return hi, (x - hi.astype(F32)).astype(BF16)


def _router_body(x_ref, g_ref, sh_ref, sc_ref, rt_ref, a_ref, aff_ref):
    a = _modulated(x_ref[0], g_ref[...], sh_ref[0], sc_ref[0])
    a_ref[0] = a
    a_hi, a_lo = _split_bf16(a)
    r_hi, r_lo = _split_bf16(rt_ref[...])
    logits = _dot_nt(r_hi, a_hi) + (_dot_nt(r_hi, a_lo) + _dot_nt(r_lo, a_hi))
    e = jnp.exp(logits - jnp.max(logits, axis=0, keepdims=True))
    aff_ref[0] = e / jnp.sum(e, axis=0, keepdims=True)


def moe_router(h, g, shift, scale, router, *, tm):
    b, n, d = h.shape
    e = router.shape[1]
    return pl.pallas_call(
        _router_body,
        grid=(b, n // tm),
        in_specs=[pl.BlockSpec((1, tm, d), lambda bi, i: (bi, i, 0)),
                  pl.BlockSpec((1, d), lambda bi, i: (0, 0)),
                  pl.BlockSpec((1, 1, d), lambda bi, i: (bi, 0, 0)),
                  pl.BlockSpec((1, 1, d), lambda bi, i: (bi, 0, 0)),
                  pl.BlockSpec((e, d), lambda bi, i: (0, 0))],
        out_specs=[pl.BlockSpec((1, tm, d), lambda bi, i: (bi, i, 0)),
                   pl.BlockSpec((1, e, tm), lambda bi, i: (bi, 0, i))],
        out_shape=[jax.ShapeDtypeStruct((b, n, d), F32), jax.ShapeDtypeStruct((b, e, n), F32)],
        compiler_params=_params(("parallel", "arbitrary")),
        name="moe_router",
    )(h, g, shift, scale, router.T)


def _row_copy(a_hbm, xf_ref, sem, src_row, dst_row):
    return pltpu.make_async_copy(a_hbm.at[pl.ds(src_row, 1), :], xf_ref.at[pl.ds(dst_row, 1), :], sem.at[0])


def _ffn_body(rows_ref, next_rows_ref, a_hbm, w1_ref, w3_ref, w2_ref, gate_ref, o_ref, xf_ref, xb_ref, hid_ref,
              sem, *, nt, tf):
    e = pl.program_id(0)
    j = pl.program_id(1)
    n_steps = pl.num_programs(1)
    m = xf_ref.shape[0]
    rows_per_step = m // (nt + xf_ref.shape[1] // tf)

    def wait_all_rows():
        def wait(r, carry):
            _row_copy(a_hbm, xf_ref, sem, 0, r).wait()
            return carry

        lax.fori_loop(0, m, wait, 0, unroll=16)

    def prefetch_next_rows():
        first = j * rows_per_step
        for r in range(rows_per_step):
            _row_copy(a_hbm, xf_ref, sem, next_rows_ref[0, 0, first + r], first + r).start()

    @pl.when((e == 0) & (j == 0))
    def _():
        def start(r, carry):
            _row_copy(a_hbm, xf_ref, sem, rows_ref[0, 0, r], r).start()
            return carry

        lax.fori_loop(0, m, start, 0, unroll=8)

    @pl.when(j == 0)
    def _():
        wait_all_rows()

        def cast(i, carry):
            sl = pl.ds(pl.multiple_of(i * GATHER_CAST_ROWS, GATHER_CAST_ROWS), GATHER_CAST_ROWS)
            xb_ref[sl, :] = xf_ref[sl, :].astype(BF16)
            return carry

        lax.fori_loop(0, m // GATHER_CAST_ROWS, cast, 0)

    @pl.when(j < nt)
    def _():
        prefetch_next_rows()
        x = xb_ref[...]
        h1 = _dot(x, w1_ref[0, 0].astype(BF16))
        h3 = _dot(x, w3_ref[0, 0].astype(BF16))
        hid_ref[j] = (h1 * jax.nn.sigmoid(h1) * h3).astype(BF16)

    @pl.when(j >= nt)
    def _():
        prefetch_next_rows()
        acc = _dot(hid_ref[0], w2_ref[0, 0, 0:tf, :].astype(BF16))
        for k in range(1, nt):
            acc = acc + _dot(hid_ref[k], w2_ref[0, 0, k * tf:(k + 1) * tf, :].astype(BF16))
        o_ref[0] = (acc * gate_ref[0]).astype(o_ref.dtype)

    @pl.when((e == pl.num_programs(0) - 1) & (j == n_steps - 1))
    def _():
        wait_all_rows()


def moe_ffn(rows, a, gate, w1, w3, w2, layer, *, tf):
    e, _, m = rows.shape
    d = a.shape[1]
    f = w1.shape[3]
    nt = f // tf
    nd = d // tf
    assert m % (nt + nd) == 0
    return pl.pallas_call(
        functools.partial(_ffn_body, nt=nt, tf=tf),
        grid=(e, nt + nd),
        in_specs=[pl.BlockSpec((1, 1, m), lambda ei, j: (ei, 0, 0), memory_space=pltpu.SMEM),
                  pl.BlockSpec((1, 1, m), lambda ei, j: ((ei + 1) % e, 0, 0), memory_space=pltpu.SMEM),
                  pl.BlockSpec(memory_space=pl.ANY),
                  pl.BlockSpec((1, 1, d, tf), lambda ei, j: (layer, ei, 0, jnp.minimum(j, nt - 1))),
                  pl.BlockSpec((1, 1, d, tf), lambda ei, j: (layer, ei, 0, jnp.minimum(j, nt - 1))),
                  pl.BlockSpec((1, 1, f, tf), lambda ei, j: (layer, ei, 0, jnp.maximum(j - nt, 0))),
                  pl.BlockSpec((1, m, 1), lambda ei, j: (ei, 0, 0))],
        out_specs=pl.BlockSpec((1, m, tf), lambda ei, j: (ei, 0, jnp.maximum(j - nt, 0))),
        out_shape=jax.ShapeDtypeStruct((e, m, d), BF16),
        scratch_shapes=[pltpu.VMEM((m, d), F32), pltpu.VMEM((m, d), BF16), pltpu.VMEM((nt, m, tf), BF16),
                        pltpu.SemaphoreType.DMA((1,))],
        compiler_params=_params(("arbitrary", "arbitrary")),
        name="moe_ffn",
    )(rows, rows, a, w1, w3, w2, gate)


def prefix_constants(n):
    i = jnp.arange(LANES)
    within = (i[:, None] < i[None, :]).astype(BF16)
    tok_tile = jnp.arange(n) // LANES
    before = (tok_tile[:, None] < i[None, :]).astype(BF16)
    return within, before


def _exclusive_prefix(mask, within_ref, before_ref):
    n = mask.shape[1]
    m = mask.astype(F32).astype(BF16)
    tile_start = _dot(m, before_ref[...])
    parts = []
    for t in range(n // LANES):
        cols = slice(t * LANES, (t + 1) * LANES)
        parts.append(_dot(m[:, cols], within_ref[...]) + tile_start[:, t:t + 1])
    return jnp.concatenate(parts, axis=1), tile_start


def _select_body(aff_ref, within_ref, before_ref, pos_ref, idx_ref, gate_ref, start_ref, pos_all, start_all,
                 *, cap):
    e = pl.program_id(1)
    n = aff_ref.shape[2]

    @pl.when(e == 0)
    def _():
        bits = pltpu.bitcast(aff_ref[0], jnp.int32)

        def refine(i, thr):
            cand = thr | lax.shift_left(jnp.int32(1), 30 - i)
            cnt = jnp.sum((bits >= cand).astype(F32), axis=-1, keepdims=True)
            return jnp.where(cnt >= cap, cand, thr)

        thr = lax.fori_loop(0, 31, refine, jnp.zeros((bits.shape[0], 1), jnp.int32))
        above = bits > thr
        tied = bits == thr
        need = cap - jnp.sum(above.astype(F32), axis=-1, keepdims=True)
        tied_rank, _ = _exclusive_prefix(tied, within_ref, before_ref)
        sel = above | (tied & (tied_rank < need))
        pos, tile_start = _exclusive_prefix(sel, within_ref, before_ref)
        pos_all[...] = jnp.where(sel, pos, -1.0)
        start_all[...] = tile_start

    a = aff_ref[0, pl.ds(e, 1), :]
    pos = pos_all[pl.ds(e, 1), :]
    pos_ref[0, 0] = pos
    start_ref[0, 0] = start_all[pl.ds(e, 1), :]

    slot = lax.broadcasted_iota(jnp.int32, (cap, n), 0).astype(F32)
    onehot = jnp.where(jnp.broadcast_to(pos, (cap, n)) == slot, 1.0, 0.0).astype(BF16)
    tok = lax.broadcasted_iota(jnp.int32, (8, n), 1)
    row = lax.broadcasted_iota(jnp.int32, (8, n), 0)
    a8 = jnp.broadcast_to(a, (8, n))
    a_hi = a8.astype(BF16).astype(F32)
    a_mid = (a8 - a_hi).astype(BF16).astype(F32)
    a_lo = a8 - a_hi - a_mid
    table = jnp.where(row == 0, lax.shift_right_logical(tok, 6).astype(F32),
                      jnp.where(row == 1, (tok & 63).astype(F32),
                                jnp.where(row == 2, a_hi,
                                          jnp.where(row == 3, a_mid, jnp.where(row == 4, a_lo, 0.0)))))
    picked = _dot_nt(table.astype(BF16), onehot)
    idx_ref[0, 0] = (picked[0:1] * 64.0 + picked[1:2]).astype(jnp.int32)
    gate_ref[0, 0] = picked[2:3] + picked[3:4] + picked[4:5]


def moe_select(aff_t, cap):
    b, e, n = aff_t.shape
    within, before = prefix_constants(n)
    row = lambda last: pl.BlockSpec((1, 1, 1, last), lambda bi, ei: (bi, ei, 0, 0))
    return pl.pallas_call(
        functools.partial(_select_body, cap=cap),
        grid=(b, e),
        in_specs=[pl.BlockSpec((1, e, n), lambda bi, ei: (bi, 0, 0)),
                  pl.BlockSpec((LANES, LANES), lambda bi, ei: (0, 0)),
                  pl.BlockSpec((n, LANES), lambda bi, ei: (0, 0))],
        out_specs=[row(n), row(cap), row(cap), row(LANES)],
        out_shape=[jax.ShapeDtypeStruct((b, e, 1, n), F32), jax.ShapeDtypeStruct((b, e, 1, cap), jnp.int32),
                   jax.ShapeDtypeStruct((b, e, 1, cap), F32), jax.ShapeDtypeStruct((b, e, 1, LANES), F32)],
        scratch_shapes=[pltpu.VMEM((e, n), F32), pltpu.VMEM((e, LANES), F32)],
        compiler_params=_params(("parallel", "arbitrary")),
        name="moe_select",
    )(aff_t, within, before)


def _combine_body(start_ref, pos_ref, y_ref, h_ref, gate_ref, fg_ref, o_ref, win_ref, *, tt, final):
    bi = pl.program_id(0)
    ti = pl.program_id(1)
    n_exp, _, cap, d = y_ref.shape
    w = COMBINE_WINDOW

    los = [start_ref[bi, e, ti] for e in range(n_exp)]
    his = [start_ref[bi, e, ti + 1] for e in range(n_exp)]
    bases = [(lo // BF16_ROWS) * BF16_ROWS for lo in los]
    n_pass = functools.reduce(jnp.maximum, [(hi - base + w - 1) // w for hi, base in zip(his, bases)])
    lane = lax.broadcasted_iota(jnp.int32, (tt, w), 1)

    def one_pass(p, acc):
        hots = []
        for e in range(n_exp):
            want = bases[e] + p * w
            st = pl.multiple_of(jnp.minimum(want, cap - w), BF16_ROWS)
            win_ref[e * w:(e + 1) * w, :] = y_ref[e, 0, pl.ds(st, w), :]
            slot = st + lane
            hit = (pos_ref[0, :, e:e + 1] == slot.astype(F32)) & (slot >= want)
            hots.append(jnp.where(hit, 1.0, 0.0).astype(BF16))
        return acc + _dot(jnp.concatenate(hots, axis=1), win_ref[...])

    acc = lax.fori_loop(0, n_pass, one_pass, jnp.zeros((tt, d), F32))
    h = h_ref[0] + gate_ref[0] * acc
    if final:
        h = h * lax.rsqrt(jnp.mean(h * h, axis=-1, keepdims=True) + EPS) * fg_ref[...]
    o_ref[0] = h


def moe_combine(tile_start, pos_t, y, h, gate, final_g, *, tt, final):
    b, n, d = h.shape
    e, _, cap, _ = y.shape
    grid_spec = pltpu.PrefetchScalarGridSpec(
        num_scalar_prefetch=1,
        grid=(b, n // tt),
        in_specs=[pl.BlockSpec((1, tt, e), lambda bi, i, s: (bi, i, 0)),
                  pl.BlockSpec((e, 1, cap, d), lambda bi, i, s: (0, bi, 0, 0), pipeline_mode=pl.Buffered(1)),
                  pl.BlockSpec((1, tt, d), lambda bi, i, s: (bi, i, 0)),
                  pl.BlockSpec((1, 1, d), lambda bi, i, s: (bi, 0, 0)),
                  pl.BlockSpec((1, d), lambda bi, i, s: (0, 0))],
        out_specs=pl.BlockSpec((1, tt, d), lambda bi, i, s: (bi, i, 0)),
        scratch_shapes=[pltpu.VMEM((e * COMBINE_WINDOW, d), BF16)])
    return pl.pallas_call(
        functools.partial(_combine_body, tt=tt, final=final),
        grid_spec=grid_spec,
        out_shape=jax.ShapeDtypeStruct((b, n, d), F32),
        compiler_params=_params(("arbitrary", "arbitrary")),
        name="moe_combine",
    )(tile_start, pos_t, y, h, gate, final_g)


def _expert_choice_moe(h, g, shift, scale, gate2, router, w1, w3, w2, layer, final_g, *, final):
    b, n, d = h.shape
    e = router.shape[1]
    cap = CAPACITY_FACTOR * n // e
    a, aff_t = moe_router(h, g, shift, scale, router, tm=512)
    pos, idx, gate, start = moe_select(aff_t, cap)
    rows = idx[:, :, 0, :] + (jnp.arange(b, dtype=jnp.int32) * n)[:, None, None]
    rows = rows.transpose(1, 0, 2).reshape(e, 1, b * cap)
    gate_e = gate[:, :, 0, :].transpose(1, 0, 2).reshape(e, b * cap, 1)
    y = moe_ffn(rows, a.reshape(b * n, d), gate_e, w1, w3, w2, layer, tf=FFN_TILE)
    stride = COMBINE_TOKENS // LANES
    tile_start = start[:, :, 0, 0:n // LANES + 1:stride].astype(jnp.int32)
    pos_t = pos[:, :, 0, :].transpose(0, 2, 1)
    return moe_combine(tile_start, pos_t, y.reshape(e, b, cap, d), h, gate2, final_g,
                       tt=COMBINE_TOKENS, final=final)


def kernel(x, c, ctx, c_ctx, mod_w, mod_b, norm1_g, norm2_g, router, w1, w3, w2, attn_w_in, attn_w_out,
           na_rpb, q_norm_g, k_norm_g, sgu_w_in, sgu_w_out, sgu_ws, sgu_b, sgu_ln_g, sgu_ln_b, final_norm_g):
    b, n, d = x.shape
    cvec = jnp.concatenate([c, c_ctx[None], jnp.zeros((8 - b - 1, d), c.dtype)], axis=0)
    mods = adaln(cvec, mod_w, mod_b)
    cos, sin = rope_tables(n)
    h = x
    for l in range(DEPTH):
        mod = [mods[l, :, k * d:(k + 1) * d] for k in range(N_MOD)]
        sh1, sc1, g1, sh2, sc2, g2 = (m[:b, None, :] for m in mod)
        n1 = norm1_g[l].reshape(1, d)
        if l % 2 == 0:
            ev = l // 2
            csh1, csc1 = (jnp.broadcast_to(m[b][None, None, :], (b, 1, d)) for m in mod[:2])
            w_in = attn_w_in[ev].astype(BF16)
            proj = norm_mod_matmul(h, n1, sh1, sc1, w_in, tm=1024, tn=512)
            ctxp = norm_mod_matmul(ctx, n1, csh1, csc1, w_in[:, Q_COLS:], tm=CTX_LEN, tn=512)
            tl, tr = na_bias_tables(na_rpb[ev])
            heads_a = neighbourhood_attention(proj, ctxp, tl, tr)
            heads_b = gqa_attention(proj, ctxp, q_norm_g[ev].reshape(1, -1), k_norm_g[ev].reshape(1, -1),
                                    cos, sin, tq=2 * GQA_Q_CHUNK)
            w_out = attn_w_out[ev].astype(BF16)
            h = matmul_residual([(heads_a, w_out[:NA_W]), (heads_b, w_out[NA_W:])], h, g1, tm=1024, tn=512)
        else:
            o = l // 2
            z = norm_mod_matmul(h, n1, sh1, sc1, sgu_w_in[o].astype(BF16), tm=1024, tn=512, gelu=True)
            mixed = sgu_mix(z, sgu_ws[o], sgu_b[o], sgu_ln_g[o], sgu_ln_b[o])
            h = matmul_residual([(mixed, sgu_w_out[o].astype(BF16))], h, g1, tm=1024, tn=512)
        h = _expert_choice_moe(h, norm2_g[l].reshape(1, d), sh2, sc2, g2, router[l], w1, w3, w2, l,
                               final_norm_g.reshape(1, d), final=(l == DEPTH - 1))
    return h
```

```python
import functools

import jax
import jax.numpy as jnp
from jax import lax
from jax.experimental import pallas as pl
from jax.experimental.pallas import tpu as pltpu

D_MODEL = 2048
DEPTH = 2
GRID_W = 64
CTX_LEN = 256
HEAD_DIM = 128
ATTN_SCALE = HEAD_DIM ** -0.5
NA_HEADS = 8
NA_KH = 8
NA_KW = 16
GQA_Q_HEADS = 8
GQA_KV_HEADS = 2
GQA_GROUP = GQA_Q_HEADS // GQA_KV_HEADS
ROPE_THETA = 10000.0
CHUNK = 128
SGU_GROUPS = 8
SGU_WIDTH = 2 * D_MODEL
N_EXPERTS = 16
CAPACITY_FACTOR = 2
N_MOD = 6
EPS = 1e-6
NA_W = NA_HEADS * HEAD_DIM
Q_COLS = NA_W + GQA_Q_HEADS * HEAD_DIM

QA_BLK, QB_BLK, KA_BLK, VA_BLK, KB_BLK, VB_BLK = 0, 8, 16, 24, 32, 34
CKA_BLK, CVA_BLK, CKB_BLK, CVB_BLK = 0, 8, 16, 18

NEG_INF = -1e30
V7X_VMEM_LIMIT_BYTES = 56 * 1024 * 1024
BF16 = jnp.bfloat16
F32 = jnp.float32

NA_Q_ROWS = 4
NA_K_ROWS = 12
NA_MASKED = 2 * NA_KH - 1
NA_HEADS_PER_STEP = 4

GQA_KEY_CHUNK = 512
GQA_Q_CHUNK = 256
GQA_Q_TILE = 1024
LANES = 128
BF16_ROWS = 16
FFN_TILE = 256
GATHER_CAST_ROWS = 256
COMBINE_TOKENS = 256
COMBINE_WINDOW = 64


def _params(semantics):
    return pltpu.CompilerParams(dimension_semantics=semantics,
                                vmem_limit_bytes=V7X_VMEM_LIMIT_BYTES)


def _dot(a, b):
    return jnp.dot(a, b, preferred_element_type=F32)


def _dot_nt(a, b):
    return lax.dot_general(a, b, (((1,), (1,)), ((), ())), preferred_element_type=F32)


def _adaln_body(c_ref, w_ref, b_ref, o_ref):
    c = c_ref[...]
    s = (c * jax.nn.sigmoid(c)).astype(BF16)
    o_ref[0] = _dot(s, w_ref[0].astype(BF16)) + b_ref[0]


def adaln(cvec, mod_w, mod_b, tn=1024):
    L, d, n = mod_w.shape
    return pl.pallas_call(
        _adaln_body,
        grid=(L, n // tn),
        in_specs=[pl.BlockSpec((8, d), lambda l, j: (0, 0)),
                  pl.BlockSpec((1, d, tn), lambda l, j: (l, 0, j)),
                  pl.BlockSpec((1, 1, tn), lambda l, j: (l, 0, j))],
        out_specs=pl.BlockSpec((1, 8, tn), lambda l, j: (l, 0, j)),
        out_shape=jax.ShapeDtypeStruct((L, 8, n), F32),
        compiler_params=_params(("parallel", "arbitrary")),
        name="adaln",
    )(cvec, mod_w, mod_b.reshape(L, 1, n))


def _modulated(x, g, sh, sc):
    y = x * lax.rsqrt(jnp.mean(x * x, axis=-1, keepdims=True) + EPS) * g
    return y * (1 + sc) + sh


def _nmm_body(x_ref, g_ref, sh_ref, sc_ref, w_ref, o_ref, a_ref, *, gelu):
    @pl.when(pl.program_id(2) == 0)
    def _():
        a_ref[...] = _modulated(x_ref[0], g_ref[...], sh_ref[0], sc_ref[0]).astype(BF16)

    z = _dot(a_ref[...], w_ref[...])
    if gelu:
        z = jax.nn.gelu(z)
    o_ref[0] = z.astype(o_ref.dtype)


def norm_mod_matmul(h, g, shift, scale, w, *, tm, tn, gelu=False):
    b, n, d = h.shape
    n_out = w.shape[1]
    return pl.pallas_call(
        functools.partial(_nmm_body, gelu=gelu),
        grid=(b, n // tm, n_out // tn),
        in_specs=[pl.BlockSpec((1, tm, d), lambda bi, i, j: (bi, i, 0)),
                  pl.BlockSpec((1, d), lambda bi, i, j: (0, 0)),
                  pl.BlockSpec((1, 1, d), lambda bi, i, j: (bi, 0, 0)),
                  pl.BlockSpec((1, 1, d), lambda bi, i, j: (bi, 0, 0)),
                  pl.BlockSpec((d, tn), lambda bi, i, j: (0, j))],
        out_specs=pl.BlockSpec((1, tm, tn), lambda bi, i, j: (bi, i, j)),
        out_shape=jax.ShapeDtypeStruct((b, n, n_out), BF16),
        scratch_shapes=[pltpu.VMEM((tm, d), BF16)],
        compiler_params=_params(("parallel", "parallel", "arbitrary")),
        name="norm_mod_matmul",
    )(h, g, shift, scale, w)


def _mmr_body(*refs, n_parts):
    a_refs, w_refs = refs[:n_parts], refs[n_parts:2 * n_parts]
    h_ref, gate_ref, o_ref = refs[2 * n_parts:]
    acc = _dot(a_refs[0][0], w_refs[0][...])
    for a_ref, w_ref in zip(a_refs[1:], w_refs[1:]):
        acc = acc + _dot(a_ref[0], w_ref[...])
    o_ref[0] = h_ref[0] + gate_ref[0] * acc


def matmul_residual(parts, h, gate, *, tm, tn):
    b, n, d = h.shape
    a_list = [a for a, _ in parts]
    w_list = [w for _, w in parts]
    return pl.pallas_call(
        functools.partial(_mmr_body, n_parts=len(parts)),
        grid=(b, n // tm, d // tn),
        in_specs=([pl.BlockSpec((1, tm, a.shape[2]), lambda bi, i, j: (bi, i, 0)) for a in a_list]
                  + [pl.BlockSpec((w.shape[0], tn), lambda bi, i, j: (0, j)) for w in w_list]
                  + [pl.BlockSpec((1, tm, tn), lambda bi, i, j: (bi, i, j)),
                     pl.BlockSpec((1, 1, tn), lambda bi, i, j: (bi, 0, j))]),
        out_specs=pl.BlockSpec((1, tm, tn), lambda bi, i, j: (bi, i, j)),
        out_shape=jax.ShapeDtypeStruct((b, n, d), F32),
        compiler_params=_params(("parallel", "parallel", "arbitrary")),
        name="matmul_residual",
    )(*a_list, *w_list, h, gate)


def na_bias_tables(rpb):
    cols = jnp.arange(GRID_W)
    start = jnp.clip(cols - NA_KW // 2, 0, GRID_W - NA_KW)
    j = jnp.arange(GRID_W)
    inside = (j[None, :] >= start[:, None]) & (j[None, :] < start[:, None] + NA_KW)
    dcol = jnp.clip(j[None, :] - cols[:, None] + NA_KW - 1, 0, 2 * NA_KW - 2)
    t = jnp.where(inside[None, None], rpb[:, :, dcol], NEG_INF)
    t = jnp.concatenate([t, jnp.full_like(t[:, :1], NEG_INF)], axis=1)
    z = jnp.zeros_like(t)
    return jnp.concatenate([t, z], axis=-1), jnp.concatenate([z, t], axis=-1)


def _na_body(q_ref, k_ref, v_ref, kc_ref, vc_ref, tl_ref, tr_ref, o_ref, s_ref):
    rows = k_ref.shape[1] // GRID_W
    r0 = pl.program_id(2) * NA_Q_ROWS
    k0 = jnp.clip(r0 - NA_KH // 2, 0, rows - NA_K_ROWS)
    kstart = pl.multiple_of(k0 * GRID_W, GRID_W)
    keys = pl.ds(kstart, NA_K_ROWS * GRID_W)

    def table_index(kr, r, rs):
        valid = (kr >= rs) & (kr < rs + NA_KH)
        return jnp.where(valid, kr - r + NA_KH - 1, NA_MASKED)

    table_ids = []
    for qr in range(NA_Q_ROWS):
        r = r0 + qr
        rs = jnp.clip(r - NA_KH // 2, 0, rows - NA_KH)
        table_ids.append([(table_index(k0 + 2 * p, r, rs), table_index(k0 + 2 * p + 1, r, rs))
                          for p in range(NA_K_ROWS // 2)])

    for hh in range(NA_HEADS_PER_STEP):
        lanes = slice(hh * HEAD_DIM, (hh + 1) * HEAD_DIM)
        q = q_ref[0, :, lanes]
        s_ref[hh] = _dot_nt(q, k_ref[0, keys, lanes]) * ATTN_SCALE
        s_ctx = _dot_nt(q, kc_ref[0, :, lanes]) * ATTN_SCALE
        for qr in range(NA_Q_ROWS):
            for p in range(NA_K_ROWS // 2):
                ia, ib = table_ids[qr][p]
                blk = (hh, slice(qr * GRID_W, (qr + 1) * GRID_W), slice(p * 2 * GRID_W, (p + 1) * 2 * GRID_W))
                s_ref[blk] = s_ref[blk] + tl_ref[hh, ia] + tr_ref[hh, ib]
        s_loc = s_ref[hh]
        m = jnp.maximum(jnp.max(s_loc, axis=-1, keepdims=True), jnp.max(s_ctx, axis=-1, keepdims=True))
        p_loc = jnp.exp(s_loc - m)
        p_ctx = jnp.exp(s_ctx - m)
        denom = jnp.sum(p_loc, axis=-1, keepdims=True) + jnp.sum(p_ctx, axis=-1, keepdims=True)
        o = _dot(p_loc.astype(BF16), v_ref[0, keys, lanes]) + _dot(p_ctx.astype(BF16), vc_ref[0, :, lanes])
        o_ref[0, :, lanes] = (o / denom).astype(o_ref.dtype)


def neighbourhood_attention(proj, ctxp, tl, tr):
    b, n, _ = proj.shape
    lc = ctxp.shape[1]
    tq = NA_Q_ROWS * GRID_W
    hs = NA_HEADS_PER_STEP
    w = hs * HEAD_DIM
    table = pl.BlockSpec((hs, 2 * NA_KH, GRID_W, 2 * GRID_W), lambda bi, h, t: (h, 0, 0, 0))
    return pl.pallas_call(
        _na_body,
        grid=(b, NA_HEADS // hs, n // tq),
        in_specs=[pl.BlockSpec((1, tq, w), lambda bi, h, t: (bi, t, QA_BLK // hs + h)),
                  pl.BlockSpec((1, n, w), lambda bi, h, t: (bi, 0, KA_BLK // hs + h)),
                  pl.BlockSpec((1, n, w), lambda bi, h, t: (bi, 0, VA_BLK // hs + h)),
                  pl.BlockSpec((1, lc, w), lambda bi, h, t: (bi, 0, CKA_BLK // hs + h)),
                  pl.BlockSpec((1, lc, w), lambda bi, h, t: (bi, 0, CVA_BLK // hs + h)),
                  table, table],
        out_specs=pl.BlockSpec((1, tq, w), lambda bi, h, t: (bi, t, h)),
        out_shape=jax.ShapeDtypeStruct((b, n, NA_W), BF16),
        scratch_shapes=[pltpu.VMEM((hs, tq, NA_K_ROWS * GRID_W), F32)],
        compiler_params=_params(("parallel", "parallel", "arbitrary")),
        name="neighbourhood_attention",
    )(proj, proj, proj, ctxp, ctxp, tl, tr)


def rope_tables(n):
    t = jnp.arange(n)
    row = (t // GRID_W).astype(F32)
    col = (t % GRID_W).astype(F32)
    axis_dims = HEAD_DIM // 2
    inv_freq = ROPE_THETA ** (-jnp.arange(0, axis_dims, 2, dtype=F32) / axis_dims)
    ang_r = row[:, None] * inv_freq
    ang_c = col[:, None] * inv_freq
    cos = jnp.concatenate([jnp.cos(ang_r), jnp.cos(ang_r), jnp.cos(ang_c), jnp.cos(ang_c)], axis=-1)
    sin = jnp.concatenate([-jnp.sin(ang_r), jnp.sin(ang_r), -jnp.sin(ang_c), jnp.sin(ang_c)], axis=-1)
    return cos, sin


def _head_rms(x, g):
    return x * lax.rsqrt(jnp.mean(x * x, axis=-1, keepdims=True) + EPS) * g


def _rope(x, cos, sin):
    quarter = HEAD_DIM // 4
    lane = lax.broadcasted_iota(jnp.int32, x.shape, 1)
    partner = jnp.where(lane % (2 * quarter) < quarter,
                        pltpu.roll(x, HEAD_DIM - quarter, 1), pltpu.roll(x, quarter, 1))
    return x * cos + partner * sin


def _gqa_body(q_ref, k_ref, v_ref, kc_ref, vc_ref, qg_ref, kg_ref, cq_ref, sq_ref, ck_ref, sk_ref,
              o_ref, kt_ref):
    n = k_ref.shape[1]
    lc = kc_ref.shape[1]

    @pl.when(pl.program_id(2) == 0)
    def _():
        kg = kg_ref[...]
        for c in range(n // GQA_KEY_CHUNK):
            rows = slice(c * GQA_KEY_CHUNK, (c + 1) * GQA_KEY_CHUNK)
            kn = _rope(_head_rms(k_ref[0, rows, :].astype(F32), kg), ck_ref[rows, :], sk_ref[rows, :])
            kt_ref[:, rows] = kn.T.astype(BF16)
        kt_ref[:, n:n + lc] = _head_rms(kc_ref[0].astype(F32), kg).T.astype(BF16)

    for c in range(q_ref.shape[1] // GQA_Q_CHUNK):
        rows = slice(c * GQA_Q_CHUNK, (c + 1) * GQA_Q_CHUNK)
        q = _rope(_head_rms(q_ref[0, rows, :].astype(F32), qg_ref[...]), cq_ref[rows, :], sq_ref[rows, :])
        q = (q * ATTN_SCALE).astype(BF16)
        s = _dot(q, kt_ref[...])
        p = jnp.exp(s - jnp.max(s, axis=-1, keepdims=True))
        denom = jnp.sum(p, axis=-1, keepdims=True)
        pb = p.astype(BF16)
        o = _dot(pb[:, :n], v_ref[0]) + _dot(pb[:, n:], vc_ref[0])
        o_ref[0, rows, :] = (o / denom).astype(o_ref.dtype)


def gqa_attention(proj, ctxp, qn_g, kn_g, cos, sin, *, tq):
    b, n, _ = proj.shape
    lc = ctxp.shape[1]
    hd = HEAD_DIM
    nq = n // tq

    def qhead(kv, i):
        return kv * GQA_GROUP + i // nq

    return pl.pallas_call(
        _gqa_body,
        grid=(b, GQA_KV_HEADS, GQA_GROUP * nq),
        in_specs=[pl.BlockSpec((1, tq, hd), lambda bi, kv, i: (bi, i % nq, QB_BLK + qhead(kv, i))),
                  pl.BlockSpec((1, n, hd), lambda bi, kv, i: (bi, 0, KB_BLK + kv)),
                  pl.BlockSpec((1, n, hd), lambda bi, kv, i: (bi, 0, VB_BLK + kv)),
                  pl.BlockSpec((1, lc, hd), lambda bi, kv, i: (bi, 0, CKB_BLK + kv)),
                  pl.BlockSpec((1, lc, hd), lambda bi, kv, i: (bi, 0, CVB_BLK + kv)),
                  pl.BlockSpec((1, hd), lambda bi, kv, i: (0, 0)),
                  pl.BlockSpec((1, hd), lambda bi, kv, i: (0, 0)),
                  pl.BlockSpec((tq, hd), lambda bi, kv, i: (i % nq, 0)),
                  pl.BlockSpec((tq, hd), lambda bi, kv, i: (i % nq, 0)),
                  pl.BlockSpec((n, hd), lambda bi, kv, i: (0, 0)),
                  pl.BlockSpec((n, hd), lambda bi, kv, i: (0, 0))],
        out_specs=pl.BlockSpec((1, tq, hd), lambda bi, kv, i: (bi, i % nq, qhead(kv, i))),
        out_shape=jax.ShapeDtypeStruct((b, n, GQA_Q_HEADS * hd), BF16),
        scratch_shapes=[pltpu.VMEM((hd, n + lc), BF16)],
        compiler_params=_params(("parallel", "parallel", "arbitrary")),
        name="gqa_attention",
    )(proj, proj, proj, ctxp, ctxp, qn_g, kn_g, cos, sin, cos, sin)


def _sgu_body(z_ref, ws_ref, bs_ref, g_ref, b_ref, o_ref):
    dg = SGU_WIDTH // SGU_GROUPS
    v = z_ref[0, :, SGU_WIDTH:].astype(F32)
    mu = jnp.mean(v, axis=-1, keepdims=True)
    vc = v - mu
    var = jnp.mean(vc * vc, axis=-1, keepdims=True)
    vn = (vc * lax.rsqrt(var + EPS) * g_ref[...] + b_ref[...]).astype(BF16)
    for g in range(SGU_GROUPS):
        cols = slice(g * dg, (g + 1) * dg)
        mixed = _dot(ws_ref[g].astype(BF16), vn[:, cols]) + bs_ref[:, g:g + 1]
        o_ref[0, :, cols] = (z_ref[0, :, cols].astype(F32) * mixed).astype(o_ref.dtype)


def sgu_mix(z, ws, bs, ln_g, ln_b):
    b, n, _ = z.shape
    return pl.pallas_call(
        _sgu_body,
        grid=(b, n // CHUNK),
        in_specs=[pl.BlockSpec((1, CHUNK, 2 * SGU_WIDTH), lambda bi, c: (bi, c, 0)),
                  pl.BlockSpec((SGU_GROUPS, CHUNK, CHUNK), lambda bi, c: (0, 0, 0)),
                  pl.BlockSpec((CHUNK, SGU_GROUPS), lambda bi, c: (0, 0)),
                  pl.BlockSpec((1, SGU_WIDTH), lambda bi, c: (0, 0)),
                  pl.BlockSpec((1, SGU_WIDTH), lambda bi, c: (0, 0))],
        out_specs=pl.BlockSpec((1, CHUNK, SGU_WIDTH), lambda bi, c: (bi, c, 0)),
        out_shape=jax.ShapeDtypeStruct((b, n, SGU_WIDTH), BF16),
        compiler_params=_params(("parallel", "arbitrary")),
        name="sgu_mix",
    )(z, ws, bs.T, ln_g.reshape(1, -1), ln_b.reshape(1, -1))


def _split_bf16(x):
    hi = x.astype(BF16)
    return hi, (x - hi.astype(F32)).astype(BF16)


def _router_body(x_ref, g_ref, sh_ref, sc_ref, rt_ref, a_ref, aff_ref):
    a = _modulated(x_ref[0], g_ref[...], sh_ref[0], sc_ref[0])
    a_ref[0] = a
    a_hi, a_lo = _split_bf16(a)
    r_hi, r_lo = _split_bf16(rt_ref[...])
    logits = _dot_nt(r_hi, a_hi) + (_dot_nt(r_hi, a_lo) + _dot_nt(r_lo, a_hi))
    e = jnp.exp(logits - jnp.max(logits, axis=0, keepdims=True))
    aff_ref[0] = e / jnp.sum(e, axis=0, keepdims=True)


def moe_router(h, g, shift, scale, router, *, tm):
    b, n, d = h.shape
    e = router.shape[1]
    return pl.pallas_call(
        _router_body,
        grid=(b, n // tm),
        in_specs=[pl.BlockSpec((1, tm, d), lambda bi, i: (bi, i, 0)),
                  pl.BlockSpec((1, d), lambda bi, i: (0, 0)),
                  pl.BlockSpec((1, 1, d), lambda bi, i: (bi, 0, 0)),
                  pl.BlockSpec((1, 1, d), lambda bi, i: (bi, 0, 0)),
                  pl.BlockSpec((e, d), lambda bi, i: (0, 0))],
        out_specs=[pl.BlockSpec((1, tm, d), lambda bi, i: (bi, i, 0)),
                   pl.BlockSpec((1, e, tm), lambda bi, i: (bi, 0, i))],
        out_shape=[jax.ShapeDtypeStruct((b, n, d), F32), jax.ShapeDtypeStruct((b, e, n), F32)],
        compiler_params=_params(("parallel", "arbitrary")),
        name="moe_router",
    )(h, g, shift, scale, router.T)


def _row_copy(a_hbm, xf_ref, sem, src_row, dst_row):
    return pltpu.make_async_copy(a_hbm.at[pl.ds(src_row, 1), :], xf_ref.at[pl.ds(dst_row, 1), :], sem.at[0])


def _ffn_body(rows_ref, next_rows_ref, a_hbm, w1_ref, w3_ref, w2_ref, gate_ref, o_ref, xf_ref, xb_ref, hid_ref,
              sem, *, nt, tf):
    e = pl.program_id(0)
    j = pl.program_id(1)
    n_steps = pl.num_programs(1)
    m = xf_ref.shape[0]
    rows_per_step = m // (nt + xf_ref.shape[1] // tf)

    def wait_all_rows():
        def wait(r, carry):
            _row_copy(a_hbm, xf_ref, sem, 0, r).wait()
            return carry

        lax.fori_loop(0, m, wait, 0, unroll=16)

    def prefetch_next_rows():
        first = j * rows_per_step
        for r in range(rows_per_step):
            _row_copy(a_hbm, xf_ref, sem, next_rows_ref[0, 0, first + r], first + r).start()

    @pl.when((e == 0) & (j == 0))
    def _():
        def start(r, carry):
            _row_copy(a_hbm, xf_ref, sem, rows_ref[0, 0, r], r).start()
            return carry

        lax.fori_loop(0, m, start, 0, unroll=8)

    @pl.when(j == 0)
    def _():
        wait_all_rows()

        def cast(i, carry):
            sl = pl.ds(pl.multiple_of(i * GATHER_CAST_ROWS, GATHER_CAST_ROWS), GATHER_CAST_ROWS)
            xb_ref[sl, :] = xf_ref[sl, :].astype(BF16)
            return carry

        lax.fori_loop(0, m // GATHER_CAST_ROWS, cast, 0)

    @pl.when(j < nt)
    def _():
        prefetch_next_rows()
        x = xb_ref[...]
        h1 = _dot(x, w1_ref[0, 0].astype(BF16))
        h3 = _dot(x, w3_ref[0, 0].astype(BF16))
        hid_ref[j] = (h1 * jax.nn.sigmoid(h1) * h3).astype(BF16)

    @pl.when(j >= nt)
    def _():
        prefetch_next_rows()
        acc = _dot(hid_ref[0], w2_ref[0, 0, 0:tf, :].astype(BF16))
        for k in range(1, nt):
            acc = acc + _dot(hid_ref[k], w2_ref[0, 0, k * tf:(k + 1) * tf, :].astype(BF16))
        o_ref[0] = (acc * gate_ref[0]).astype(o_ref.dtype)

    @pl.when((e == pl.num_programs(0) - 1) & (j == n_steps - 1))
    def _():
        wait_all_rows()


def moe_ffn(rows, a, gate, w1, w3, w2, layer, *, tf):
    e, _, m = rows.shape
    d = a.shape[1]
    f = w1.shape[3]
    nt = f // tf
    nd = d // tf
    assert m % (nt + nd) == 0
    return pl.pallas_call(
        functools.partial(_ffn_body, nt=nt, tf=tf),
        grid=(e, nt + nd),
        in_specs=[pl.BlockSpec((1, 1, m), lambda ei, j: (ei, 0, 0), memory_space=pltpu.SMEM),
                  pl.BlockSpec((1, 1, m), lambda ei, j: ((ei + 1) % e, 0, 0), memory_space=pltpu.SMEM),
                  pl.BlockSpec(memory_space=pl.ANY),
                  pl.BlockSpec((1, 1, d, tf), lambda ei, j: (layer, ei, 0, jnp.minimum(j, nt - 1))),
                  pl.BlockSpec((1, 1, d, tf), lambda ei, j: (layer, ei, 0, jnp.minimum(j, nt - 1))),
                  pl.BlockSpec((1, 1, f, tf), lambda ei, j: (layer, ei, 0, jnp.maximum(j - nt, 0))),
                  pl.BlockSpec((1, m, 1), lambda ei, j: (ei, 0, 0))],
        out_specs=pl.BlockSpec((1, m, tf), lambda ei, j: (ei, 0, jnp.maximum(j - nt, 0))),
        out_shape=jax.ShapeDtypeStruct((e, m, d), BF16),
        scratch_shapes=[pltpu.VMEM((m, d), F32), pltpu.VMEM((m, d), BF16), pltpu.VMEM((nt, m, tf), BF16),
                        pltpu.SemaphoreType.DMA((1,))],
        compiler_params=_params(("arbitrary", "arbitrary")),
        name="moe_ffn",
    )(rows, rows, a, w1, w3, w2, gate)


def prefix_constants(n):
    i = jnp.arange(LANES)
    within = (i[:, None] < i[None, :]).astype(BF16)
    tok_tile = jnp.arange(n) // LANES
    before = (tok_tile[:, None] < i[None, :]).astype(BF16)
    return within, before


def _exclusive_prefix(mask, within_ref, before_ref):
    n = mask.shape[1]
    m = mask.astype(F32).astype(BF16)
    tile_start = _dot(m, before_ref[...])
    parts = []
    for t in range(n // LANES):
        cols = slice(t * LANES, (t + 1) * LANES)
        parts.append(_dot(m[:, cols], within_ref[...]) + tile_start[:, t:t + 1])
    return jnp.concatenate(parts, axis=1), tile_start


def _select_body(aff_ref, within_ref, before_ref, pos_ref, idx_ref, gate_ref, start_ref, pos_all, start_all,
                 *, cap):
    e = pl.program_id(1)
    n = aff_ref.shape[2]

    @pl.when(e == 0)
    def _():
        bits = pltpu.bitcast(aff_ref[0], jnp.int32)

        def refine(i, thr):
            cand = thr | lax.shift_left(jnp.int32(1), 30 - i)
            cnt = jnp.sum((bits >= cand).astype(F32), axis=-1, keepdims=True)
            return jnp.where(cnt >= cap, cand, thr)

        thr = lax.fori_loop(0, 31, refine, jnp.zeros((bits.shape[0], 1), jnp.int32))
        above = bits > thr
        tied = bits == thr
        need = cap - jnp.sum(above.astype(F32), axis=-1, keepdims=True)
        tied_rank, _ = _exclusive_prefix(tied, within_ref, before_ref)
        sel = above | (tied & (tied_rank < need))
        pos, tile_start = _exclusive_prefix(sel, within_ref, before_ref)
        pos_all[...] = jnp.where(sel, pos, -1.0)
        start_all[...] = tile_start

    a = aff_ref[0, pl.ds(e, 1), :]
    pos = pos_all[pl.ds(e, 1), :]
    pos_ref[0, 0] = pos
    start_ref[0, 0] = start_all[pl.ds(e, 1), :]

    slot = lax.broadcasted_iota(jnp.int32, (cap, n), 0).astype(F32)
    onehot = jnp.where(jnp.broadcast_to(pos, (cap, n)) == slot, 1.0, 0.0).astype(BF16)
    tok = lax.broadcasted_iota(jnp.int32, (8, n), 1)
    row = lax.broadcasted_iota(jnp.int32, (8, n), 0)
    a8 = jnp.broadcast_to(a, (8, n))
    a_hi = a8.astype(BF16).astype(F32)
    a_mid = (a8 - a_hi).astype(BF16).astype(F32)
    a_lo = a8 - a_hi - a_mid
    table = jnp.where(row == 0, lax.shift_right_logical(tok, 6).astype(F32),
                      jnp.where(row == 1, (tok & 63).astype(F32),
                                jnp.where(row == 2, a_hi,
                                          jnp.where(row == 3, a_mid, jnp.where(row == 4, a_lo, 0.0)))))
    picked = _dot_nt(table.astype(BF16), onehot)
    idx_ref[0, 0] = (picked[0:1] * 64.0 + picked[1:2]).astype(jnp.int32)
    gate_ref[0, 0] = picked[2:3] + picked[3:4] + picked[4:5]


def moe_select(aff_t, cap):
    b, e, n = aff_t.shape
    within, before = prefix_constants(n)
    row = lambda last: pl.BlockSpec((1, 1, 1, last), lambda bi, ei: (bi, ei, 0, 0))
    return pl.pallas_call(
        functools.partial(_select_body, cap=cap),
        grid=(b, e),
        in_specs=[pl.BlockSpec((1, e, n), lambda bi, ei: (bi, 0, 0)),
                  pl.BlockSpec((LANES, LANES), lambda bi, ei: (0, 0)),
                  pl.BlockSpec((n, LANES), lambda bi, ei: (0, 0))],
        out_specs=[row(n), row(cap), row(cap), row(LANES)],
        out_shape=[jax.ShapeDtypeStruct((b, e, 1, n), F32), jax.ShapeDtypeStruct((b, e, 1, cap), jnp.int32),
                   jax.ShapeDtypeStruct((b, e, 1, cap), F32), jax.ShapeDtypeStruct((b, e, 1, LANES), F32)],
        scratch_shapes=[pltpu.VMEM((e, n), F32), pltpu.VMEM((e, LANES), F32)],
        compiler_params=_params(("parallel", "arbitrary")),
        name="moe_select",
    )(aff_t, within, before)


def _combine_body(start_ref, pos_ref, y_ref, h_ref, gate_ref, fg_ref, o_ref, win_ref, *, tt, final):
    bi = pl.program_id(0)
    ti = pl.program_id(1)
    n_exp, _, cap, d = y_ref.shape
    w = COMBINE_WINDOW

    los = [start_ref[bi, e, ti] for e in range(n_exp)]
    his = [start_ref[bi, e, ti + 1] for e in range(n_exp)]
    bases = [(lo // BF16_ROWS) * BF16_ROWS for lo in los]
    n_pass = functools.reduce(jnp.maximum, [(hi - base + w - 1) // w for hi, base in zip(his, bases)])
    lane = lax.broadcasted_iota(jnp.int32, (1, 2 * w), 1)
    left = lane < w

    def window_sum(p):
        hots = []
        for e in range(0, n_exp, 2):
            wants, starts = [], []
            for k in range(2):
                want = bases[e + k] + p * w
                st = pl.multiple_of(jnp.minimum(want, cap - w), BF16_ROWS)
                win_ref[(e + k) * w:(e + k + 1) * w, :] = y_ref[e + k, 0, pl.ds(st, w), :]
                wants.append(want)
                starts.append(st)
            slot = jnp.where(left, starts[0] + lane, starts[1] + lane - w)
            slot = jnp.where(slot >= jnp.where(left, wants[0], wants[1]), slot, -2).astype(F32)
            tok_pos = jnp.where(left, pos_ref[0, :, e:e + 1], pos_ref[0, :, e + 1:e + 2])
            hots.append(jnp.where(tok_pos == slot, 1.0, 0.0).astype(BF16))
        return _dot(jnp.concatenate(hots, axis=1), win_ref[...])

    acc = lax.fori_loop(1, n_pass, lambda p, a: a + window_sum(p), window_sum(0))
    h = h_ref[0] + gate_ref[0] * acc
    if final:
        h = h * lax.rsqrt(jnp.mean(h * h, axis=-1, keepdims=True) + EPS) * fg_ref[...]
    o_ref[0] = h


def moe_combine(tile_start, pos_t, y, h, gate, final_g, *, tt, final):
    b, n, d = h.shape
    e, _, cap, _ = y.shape
    grid_spec = pltpu.PrefetchScalarGridSpec(
        num_scalar_prefetch=1,
        grid=(b, n // tt),
        in_specs=[pl.BlockSpec((1, tt, e), lambda bi, i, s: (bi, i, 0)),
                  pl.BlockSpec((e, 1, cap, d), lambda bi, i, s: (0, bi, 0, 0), pipeline_mode=pl.Buffered(1)),
                  pl.BlockSpec((1, tt, d), lambda bi, i, s: (bi, i, 0)),
                  pl.BlockSpec((1, 1, d), lambda bi, i, s: (bi, 0, 0)),
                  pl.BlockSpec((1, d), lambda bi, i, s: (0, 0))],
        out_specs=pl.BlockSpec((1, tt, d), lambda bi, i, s: (bi, i, 0)),
        scratch_shapes=[pltpu.VMEM((e * COMBINE_WINDOW, d), BF16)])
    return pl.pallas_call(
        functools.partial(_combine_body, tt=tt, final=final),
        grid_spec=grid_spec,
        out_shape=jax.ShapeDtypeStruct((b, n, d), F32),
        compiler_params=_params(("arbitrary", "arbitrary")),
        name="moe_combine",
    )(tile_start, pos_t, y, h, gate, final_g)


def _expert_choice_moe(h, g, shift, scale, gate2, router, w1, w3, w2, layer, final_g, *, final):
    b, n, d = h.shape
    e = router.shape[1]
    cap = CAPACITY_FACTOR * n // e
    a, aff_t = moe_router(h, g, shift, scale, router, tm=512)
    pos, idx, gate, start = moe_select(aff_t, cap)
    rows = idx[:, :, 0, :] + (jnp.arange(b, dtype=jnp.int32) * n)[:, None, None]
    rows = rows.transpose(1, 0, 2).reshape(e, 1, b * cap)
    gate_e = gate[:, :, 0, :].transpose(1, 0, 2).reshape(e, b * cap, 1)
    y = moe_ffn(rows, a.reshape(b * n, d), gate_e, w1, w3, w2, layer, tf=FFN_TILE)
    stride = COMBINE_TOKENS // LANES
    tile_start = start[:, :, 0, 0:n // LANES + 1:stride].astype(jnp.int32)
    pos_t = pos[:, :, 0, :].transpose(0, 2, 1)
    return moe_combine(tile_start, pos_t, y.reshape(e, b, cap, d), h, gate2, final_g,
                       tt=COMBINE_TOKENS, final=final)


def kernel(x, c, ctx, c_ctx, mod_w, mod_b, norm1_g, norm2_g, router, w1, w3, w2, attn_w_in, attn_w_out,
           na_rpb, q_norm_g, k_norm_g, sgu_w_in, sgu_w_out, sgu_ws, sgu_b, sgu_ln_g, sgu_ln_b, final_norm_g):
    b, n, d = x.shape
    cvec = jnp.concatenate([c, c_ctx[None], jnp.zeros((8 - b - 1, d), c.dtype)], axis=0)
    mods = adaln(cvec, mod_w, mod_b)
    cos, sin = rope_tables(n)
    h = x
    for l in range(DEPTH):
        mod = [mods[l, :, k * d:(k + 1) * d] for k in range(N_MOD)]
        sh1, sc1, g1, sh2, sc2, g2 = (m[:b, None, :] for m in mod)
        n1 = norm1_g[l].reshape(1, d)
        if l % 2 == 0:
            ev = l // 2
            csh1, csc1 = (jnp.broadcast_to(m[b][None, None, :], (b, 1, d)) for m in mod[:2])
            w_in = attn_w_in[ev].astype(BF16)
            proj = norm_mod_matmul(h, n1, sh1, sc1, w_in, tm=1024, tn=512)
            ctxp = norm_mod_matmul(ctx, n1, csh1, csc1, w_in[:, Q_COLS:], tm=CTX_LEN, tn=512)
            tl, tr = na_bias_tables(na_rpb[ev])
            heads_a = neighbourhood_attention(proj, ctxp, tl, tr)
            heads_b = gqa_attention(proj, ctxp, q_norm_g[ev].reshape(1, -1), k_norm_g[ev].reshape(1, -1),
                                    cos, sin, tq=GQA_Q_TILE)
            w_out = attn_w_out[ev].astype(BF16)
            h = matmul_residual([(heads_a, w_out[:NA_W]), (heads_b, w_out[NA_W:])], h, g1, tm=1024, tn=512)
        else:
            o = l // 2
            z = norm_mod_matmul(h, n1, sh1, sc1, sgu_w_in[o].astype(BF16), tm=1024, tn=512, gelu=True)
            mixed = sgu_mix(z, sgu_ws[o], sgu_b[o], sgu_ln_g[o], sgu_ln_b[o])
            h = matmul_residual([(mixed, sgu_w_out[o].astype(BF16))], h, g1, tm=1024, tn=512)
        h = _expert_choice_moe(h, norm2_g[l].reshape(1, d), sh2, sc2, g2, router[l], w1, w3, w2, l,
                               final_norm_g.reshape(1, d), final=(l == DEPTH - 1))
    return h
```

```python
import functools

import jax
import jax.numpy as jnp
from jax import lax
from jax.experimental import pallas as pl
from jax.experimental.pallas import tpu as pltpu

D_MODEL = 2048
DEPTH = 2
GRID_W = 64
CTX_LEN = 256
HEAD_DIM = 128
ATTN_SCALE = HEAD_DIM ** -0.5
NA_HEADS = 8
NA_KH = 8
NA_KW = 16
GQA_Q_HEADS = 8
GQA_KV_HEADS = 2
GQA_GROUP = GQA_Q_HEADS // GQA_KV_HEADS
ROPE_THETA = 10000.0
CHUNK = 128
SGU_GROUPS = 8
SGU_WIDTH = 2 * D_MODEL
N_EXPERTS = 16
CAPACITY_FACTOR = 2
N_MOD = 6
EPS = 1e-6
NA_W = NA_HEADS * HEAD_DIM
Q_COLS = NA_W + GQA_Q_HEADS * HEAD_DIM

QA_BLK, QB_BLK, KA_BLK, VA_BLK, KB_BLK, VB_BLK = 0, 8, 16, 24, 32, 34
CKA_BLK, CVA_BLK, CKB_BLK, CVB_BLK = 0, 8, 16, 18

NEG_INF = -1e30
V7X_VMEM_LIMIT_BYTES = 56 * 1024 * 1024
BF16 = jnp.bfloat16
F32 = jnp.float32

NA_Q_ROWS = 4
NA_K_ROWS = 12
NA_MASKED = 2 * NA_KH - 1
LOG2_E = 1.4426950408889634
NA_HEADS_PER_STEP = 4

GQA_KEY_CHUNK = 512
GQA_Q_CHUNK = 256
GQA_Q_TILE = 1024
LANES = 128
BF16_ROWS = 16
FFN_TILE = 256
GATHER_CAST_ROWS = 256
COMBINE_TOKENS = 256
COMBINE_WINDOW = 64
MATMUL_ROWS = 1024
MATMUL_MAX_COLS = 1536
RESIDUAL_COLS = 512
MXU_COLS = 256


def _col_tile(n_out):
    return max(t for t in range(MXU_COLS, MATMUL_MAX_COLS + 1, MXU_COLS) if n_out % t == 0)


def _params(semantics):
    return pltpu.CompilerParams(dimension_semantics=semantics,
                                vmem_limit_bytes=V7X_VMEM_LIMIT_BYTES)


def _dot(a, b):
    return jnp.dot(a, b, preferred_element_type=F32)


def _dot_nt(a, b):
    return lax.dot_general(a, b, (((1,), (1,)), ((), ())), preferred_element_type=F32)


def _adaln_body(c_ref, w_ref, b_ref, o_ref):
    c = c_ref[...]
    s = (c * jax.nn.sigmoid(c)).astype(BF16)
    o_ref[0] = _dot(s, w_ref[0].astype(BF16)) + b_ref[0]


def adaln(cvec, mod_w, mod_b, tn=1024):
    L, d, n = mod_w.shape
    return pl.pallas_call(
        _adaln_body,
        grid=(L, n // tn),
        in_specs=[pl.BlockSpec((8, d), lambda l, j: (0, 0)),
                  pl.BlockSpec((1, d, tn), lambda l, j: (l, 0, j)),
                  pl.BlockSpec((1, 1, tn), lambda l, j: (l, 0, j))],
        out_specs=pl.BlockSpec((1, 8, tn), lambda l, j: (l, 0, j)),
        out_shape=jax.ShapeDtypeStruct((L, 8, n), F32),
        compiler_params=_params(("parallel", "arbitrary")),
        name="adaln",
    )(cvec, mod_w, mod_b.reshape(L, 1, n))


def _modulated(x, g, sh, sc):
    y = x * lax.rsqrt(jnp.mean(x * x, axis=-1, keepdims=True) + EPS) * g
    return y * (1 + sc) + sh


def _nmm_body(x_ref, g_ref, sh_ref, sc_ref, w_ref, o_ref, a_ref, *, gelu):
    @pl.when(pl.program_id(2) == 0)
    def _():
        a_ref[...] = _modulated(x_ref[0], g_ref[...], sh_ref[0], sc_ref[0]).astype(BF16)

    z = _dot(a_ref[...], w_ref[...])
    if gelu:
        z = jax.nn.gelu(z)
    o_ref[0] = z.astype(o_ref.dtype)


def norm_mod_matmul(h, g, shift, scale, w, *, gelu=False):
    b, n, d = h.shape
    n_out = w.shape[1]
    tm = min(n, MATMUL_ROWS)
    tn = _col_tile(n_out)
    return pl.pallas_call(
        functools.partial(_nmm_body, gelu=gelu),
        grid=(b, n // tm, n_out // tn),
        in_specs=[pl.BlockSpec((1, tm, d), lambda bi, i, j: (bi, i, 0)),
                  pl.BlockSpec((1, d), lambda bi, i, j: (0, 0)),
                  pl.BlockSpec((1, 1, d), lambda bi, i, j: (bi, 0, 0)),
                  pl.BlockSpec((1, 1, d), lambda bi, i, j: (bi, 0, 0)),
                  pl.BlockSpec((d, tn), lambda bi, i, j: (0, j))],
        out_specs=pl.BlockSpec((1, tm, tn), lambda bi, i, j: (bi, i, j)),
        out_shape=jax.ShapeDtypeStruct((b, n, n_out), BF16),
        scratch_shapes=[pltpu.VMEM((tm, d), BF16)],
        compiler_params=_params(("parallel", "parallel", "arbitrary")),
        name="norm_mod_matmul",
    )(h, g, shift, scale, w)


def _mmr_body(*refs, n_parts):
    a_refs, w_refs = refs[:n_parts], refs[n_parts:2 * n_parts]
    h_ref, gate_ref, o_ref = refs[2 * n_parts:]
    acc = _dot(a_refs[0][0], w_refs[0][...])
    for a_ref, w_ref in zip(a_refs[1:], w_refs[1:]):
        acc = acc + _dot(a_ref[0], w_ref[...])
    o_ref[0] = h_ref[0] + gate_ref[0] * acc


def matmul_residual(parts, h, gate):
    b, n, d = h.shape
    tm = min(n, MATMUL_ROWS)
    tn = RESIDUAL_COLS
    a_list = [a for a, _ in parts]
    w_list = [w for _, w in parts]
    return pl.pallas_call(
        functools.partial(_mmr_body, n_parts=len(parts)),
        grid=(b, n // tm, d // tn),
        in_specs=([pl.BlockSpec((1, tm, a.shape[2]), lambda bi, i, j: (bi, i, 0)) for a in a_list]
                  + [pl.BlockSpec((w.shape[0], tn), lambda bi, i, j: (0, j)) for w in w_list]
                  + [pl.BlockSpec((1, tm, tn), lambda bi, i, j: (bi, i, j)),
                     pl.BlockSpec((1, 1, tn), lambda bi, i, j: (bi, 0, j))]),
        out_specs=pl.BlockSpec((1, tm, tn), lambda bi, i, j: (bi, i, j)),
        out_shape=jax.ShapeDtypeStruct((b, n, d), F32),
        compiler_params=_params(("parallel", "parallel", "arbitrary")),
        name="matmul_residual",
    )(*a_list, *w_list, h, gate)


def na_bias_tables(rpb):
    cols = jnp.arange(GRID_W)
    start = jnp.clip(cols - NA_KW // 2, 0, GRID_W - NA_KW)
    j = jnp.arange(GRID_W)
    inside = (j[None, :] >= start[:, None]) & (j[None, :] < start[:, None] + NA_KW)
    dcol = jnp.clip(j[None, :] - cols[:, None] + NA_KW - 1, 0, 2 * NA_KW - 2)
    t = jnp.where(inside[None, None], rpb[:, :, dcol] * LOG2_E, NEG_INF)
    t = jnp.concatenate([t, jnp.full_like(t[:, :1], NEG_INF)], axis=1)
    z = jnp.zeros_like(t)
    return jnp.concatenate([t, z], axis=-1), jnp.concatenate([z, t], axis=-1)


def _na_body(q_ref, k_ref, v_ref, kc_ref, vc_ref, tl_ref, tr_ref, o_ref, s_ref):
    rows = k_ref.shape[1] // GRID_W
    r0 = pl.program_id(2) * NA_Q_ROWS
    k0 = jnp.clip(r0 - NA_KH // 2, 0, rows - NA_K_ROWS)
    kstart = pl.multiple_of(k0 * GRID_W, GRID_W)
    keys = pl.ds(kstart, NA_K_ROWS * GRID_W)
    scale = ATTN_SCALE * LOG2_E

    def table_index(kr, r, rs):
        valid = (kr >= rs) & (kr < rs + NA_KH)
        return jnp.where(valid, kr - r + NA_KH - 1, NA_MASKED)

    table_ids = []
    for qr in range(NA_Q_ROWS):
        r = r0 + qr
        rs = jnp.clip(r - NA_KH // 2, 0, rows - NA_KH)
        table_ids.append([(table_index(k0 + 2 * p, r, rs), table_index(k0 + 2 * p + 1, r, rs))
                          for p in range(NA_K_ROWS // 2)])

    for hh in range(NA_HEADS_PER_STEP):
        lanes = slice(hh * HEAD_DIM, (hh + 1) * HEAD_DIM)
        q = q_ref[0, :, lanes]
        s_ref[hh] = _dot_nt(q, k_ref[0, keys, lanes]) * scale
        s_ctx = _dot_nt(q, kc_ref[0, :, lanes]) * scale
        for qr in range(NA_Q_ROWS):
            for p in range(NA_K_ROWS // 2):
                ia, ib = table_ids[qr][p]
                blk = (hh, slice(qr * GRID_W, (qr + 1) * GRID_W), slice(p * 2 * GRID_W, (p + 1) * 2 * GRID_W))
                s_ref[blk] = s_ref[blk] + tl_ref[hh, ia] + tr_ref[hh, ib]
        s_loc = s_ref[hh]
        m = jnp.maximum(jnp.max(s_loc, axis=-1, keepdims=True), jnp.max(s_ctx, axis=-1, keepdims=True))
        p_loc = jnp.exp2(s_loc - m)
        p_ctx = jnp.exp2(s_ctx - m)
        denom = jnp.sum(p_loc, axis=-1, keepdims=True) + jnp.sum(p_ctx, axis=-1, keepdims=True)
        o = _dot(p_loc.astype(BF16), v_ref[0, keys, lanes]) + _dot(p_ctx.astype(BF16), vc_ref[0, :, lanes])
        o_ref[0, :, lanes] = (o / denom).astype(o_ref.dtype)


def neighbourhood_attention(proj, ctxp, tl, tr):
    b, n, _ = proj.shape
    lc = ctxp.shape[1]
    tq = NA_Q_ROWS * GRID_W
    hs = NA_HEADS_PER_STEP
    w = hs * HEAD_DIM
    table = pl.BlockSpec((hs, 2 * NA_KH, GRID_W, 2 * GRID_W), lambda bi, h, t: (h, 0, 0, 0))
    return pl.pallas_call(
        _na_body,
        grid=(b, NA_HEADS // hs, n // tq),
        in_specs=[pl.BlockSpec((1, tq, w), lambda bi, h, t: (bi, t, QA_BLK // hs + h)),
                  pl.BlockSpec((1, n, w), lambda bi, h, t: (bi, 0, KA_BLK // hs + h)),
                  pl.BlockSpec((1, n, w), lambda bi, h, t: (bi, 0, VA_BLK // hs + h)),
                  pl.BlockSpec((1, lc, w), lambda bi, h, t: (bi, 0, CKA_BLK // hs + h)),
                  pl.BlockSpec((1, lc, w), lambda bi, h, t: (bi, 0, CVA_BLK // hs + h)),
                  table, table],
        out_specs=pl.BlockSpec((1, tq, w), lambda bi, h, t: (bi, t, h)),
        out_shape=jax.ShapeDtypeStruct((b, n, NA_W), BF16),
        scratch_shapes=[pltpu.VMEM((hs, tq, NA_K_ROWS * GRID_W), F32)],
        compiler_params=_params(("parallel", "parallel", "arbitrary")),
        name="neighbourhood_attention",
    )(proj, proj, proj, ctxp, ctxp, tl, tr)


def rope_tables(n):
    t = jnp.arange(n)
    row = (t // GRID_W).astype(F32)
    col = (t % GRID_W).astype(F32)
    axis_dims = HEAD_DIM // 2
    inv_freq = ROPE_THETA ** (-jnp.arange(0, axis_dims, 2, dtype=F32) / axis_dims)
    ang_r = row[:, None] * inv_freq
    ang_c = col[:, None] * inv_freq
    cos = jnp.concatenate([jnp.cos(ang_r), jnp.cos(ang_r), jnp.cos(ang_c), jnp.cos(ang_c)], axis=-1)
    sin = jnp.concatenate([-jnp.sin(ang_r), jnp.sin(ang_r), -jnp.sin(ang_c), jnp.sin(ang_c)], axis=-1)
    return cos, sin


def _head_rms(x, g):
    return x * lax.rsqrt(jnp.mean(x * x, axis=-1, keepdims=True) + EPS) * g


def _rope(x, cos, sin):
    quarter = HEAD_DIM // 4
    lane = lax.broadcasted_iota(jnp.int32, x.shape, 1)
    partner = jnp.where(lane % (2 * quarter) < quarter,
                        pltpu.roll(x, HEAD_DIM - quarter, 1), pltpu.roll(x, quarter, 1))
    return x * cos + partner * sin


def _gqa_body(q_ref, k_ref, v_ref, kc_ref, vc_ref, qg_ref, kg_ref, cq_ref, sq_ref, ck_ref, sk_ref,
              o_ref, kt_ref):
    n = k_ref.shape[1]
    lc = kc_ref.shape[1]

    @pl.when(pl.program_id(2) == 0)
    def _():
        kg = kg_ref[...]
        for c in range(n // GQA_KEY_CHUNK):
            rows = slice(c * GQA_KEY_CHUNK, (c + 1) * GQA_KEY_CHUNK)
            kn = _rope(_head_rms(k_ref[0, rows, :].astype(F32), kg), ck_ref[rows, :], sk_ref[rows, :])
            kt_ref[:, rows] = kn.T.astype(BF16)
        kt_ref[:, n:n + lc] = _head_rms(kc_ref[0].astype(F32), kg).T.astype(BF16)

    for c in range(q_ref.shape[1] // GQA_Q_CHUNK):
        rows = slice(c * GQA_Q_CHUNK, (c + 1) * GQA_Q_CHUNK)
        q = _rope(_head_rms(q_ref[0, rows, :].astype(F32), qg_ref[...]), cq_ref[rows, :], sq_ref[rows, :])
        q = (q * (ATTN_SCALE * LOG2_E)).astype(BF16)
        s = _dot(q, kt_ref[...])
        p = jnp.exp2(s - jnp.max(s, axis=-1, keepdims=True))
        denom = jnp.sum(p, axis=-1, keepdims=True)
        pb = p.astype(BF16)
        o = _dot(pb[:, :n], v_ref[0]) + _dot(pb[:, n:], vc_ref[0])
        o_ref[0, rows, :] = (o / denom).astype(o_ref.dtype)


def gqa_attention(proj, ctxp, qn_g, kn_g, cos, sin, *, tq):
    b, n, _ = proj.shape
    lc = ctxp.shape[1]
    hd = HEAD_DIM
    nq = n // tq

    def qhead(kv, i):
        return kv * GQA_GROUP + i // nq

    return pl.pallas_call(
        _gqa_body,
        grid=(b, GQA_KV_HEADS, GQA_GROUP * nq),
        in_specs=[pl.BlockSpec((1, tq, hd), lambda bi, kv, i: (bi, i % nq, QB_BLK + qhead(kv, i))),
                  pl.BlockSpec((1, n, hd), lambda bi, kv, i: (bi, 0, KB_BLK + kv)),
                  pl.BlockSpec((1, n, hd), lambda bi, kv, i: (bi, 0, VB_BLK + kv)),
                  pl.BlockSpec((1, lc, hd), lambda bi, kv, i: (bi, 0, CKB_BLK + kv)),
                  pl.BlockSpec((1, lc, hd), lambda bi, kv, i: (bi, 0, CVB_BLK + kv)),
                  pl.BlockSpec((1, hd), lambda bi, kv, i: (0, 0)),
                  pl.BlockSpec((1, hd), lambda bi, kv, i: (0, 0)),
                  pl.BlockSpec((tq, hd), lambda bi, kv, i: (i % nq, 0)),
                  pl.BlockSpec((tq, hd), lambda bi, kv, i: (i % nq, 0)),
                  pl.BlockSpec((n, hd), lambda bi, kv, i: (0, 0)),
                  pl.BlockSpec((n, hd), lambda bi, kv, i: (0, 0))],
        out_specs=pl.BlockSpec((1, tq, hd), lambda bi, kv, i: (bi, i % nq, qhead(kv, i))),
        out_shape=jax.ShapeDtypeStruct((b, n, GQA_Q_HEADS * hd), BF16),
        scratch_shapes=[pltpu.VMEM((hd, n + lc), BF16)],
        compiler_params=_params(("parallel", "parallel", "arbitrary")),
        name="gqa_attention",
    )(proj, proj, proj, ctxp, ctxp, qn_g, kn_g, cos, sin, cos, sin)


def _sgu_body(z_ref, ws_ref, bs_ref, g_ref, b_ref, o_ref):
    dg = SGU_WIDTH // SGU_GROUPS
    v = z_ref[0, :, SGU_WIDTH:].astype(F32)
    mu = jnp.mean(v, axis=-1, keepdims=True)
    vc = v - mu
    var = jnp.mean(vc * vc, axis=-1, keepdims=True)
    vn = (vc * lax.rsqrt(var + EPS) * g_ref[...] + b_ref[...]).astype(BF16)
    for g in range(SGU_GROUPS):
        cols = slice(g * dg, (g + 1) * dg)
        mixed = _dot(ws_ref[g].astype(BF16), vn[:, cols]) + bs_ref[:, g:g + 1]
        o_ref[0, :, cols] = (z_ref[0, :, cols].astype(F32) * mixed).astype(o_ref.dtype)


def sgu_mix(z, ws, bs, ln_g, ln_b):
    b, n, _ = z.shape
    return pl.pallas_call(
        _sgu_body,
        grid=(b, n // CHUNK),
        in_specs=[pl.BlockSpec((1, CHUNK, 2 * SGU_WIDTH), lambda bi, c: (bi, c, 0)),
                  pl.BlockSpec((SGU_GROUPS, CHUNK, CHUNK), lambda bi, c: (0, 0, 0)),
                  pl.BlockSpec((CHUNK, SGU_GROUPS), lambda bi, c: (0, 0)),
                  pl.BlockSpec((1, SGU_WIDTH), lambda bi, c: (0, 0)),
                  pl.BlockSpec((1, SGU_WIDTH), lambda bi, c: (0, 0))],
        out_specs=pl.BlockSpec((1, CHUNK, SGU_WIDTH), lambda bi, c: (bi, c, 0)),
        out_shape=jax.ShapeDtypeStruct((b, n, SGU_WIDTH), BF16),
        compiler_params=_params(("parallel", "arbitrary")),
        name="sgu_mix",
    )(z, ws, bs.T, ln_g.reshape(1, -1), ln_b.reshape(1, -1))


def _split_bf16(x):
    hi = x.astype(BF16)
    return hi, (x - hi.astype(F32)).astype(BF16)


def _router_body(x_ref, g_ref, sh_ref, sc_ref, rt_ref, a_ref, aff_ref):
    a = _modulated(x_ref[0], g_ref[...], sh_ref[0], sc_ref[0])
    a_ref[0] = a
    a_hi, a_lo = _split_bf16(a)
    r_hi, r_lo = _split_bf16(rt_ref[...])
    logits = _dot_nt(r_hi, a_hi) + (_dot_nt(r_hi, a_lo) + _dot_nt(r_lo, a_hi))
    e = jnp.exp(logits - jnp.max(logits, axis=0, keepdims=True))
    aff_ref[0] = e / jnp.sum(e, axis=0, keepdims=True)


def moe_router(h, g, shift, scale, router, *, tm):
    b, n, d = h.shape
    e = router.shape[1]
    return pl.pallas_call(
        _router_body,
        grid=(b, n // tm),
        in_specs=[pl.BlockSpec((1, tm, d), lambda bi, i: (bi, i, 0)),
                  pl.BlockSpec((1, d), lambda bi, i: (0, 0)),
                  pl.BlockSpec((1, 1, d), lambda bi, i: (bi, 0, 0)),
                  pl.BlockSpec((1, 1, d), lambda bi, i: (bi, 0, 0)),
                  pl.BlockSpec((e, d), lambda bi, i: (0, 0))],
        out_specs=[pl.BlockSpec((1, tm, d), lambda bi, i: (bi, i, 0)),
                   pl.BlockSpec((1, e, tm), lambda bi, i: (bi, 0, i))],
        out_shape=[jax.ShapeDtypeStruct((b, n, d), F32), jax.ShapeDtypeStruct((b, e, n), F32)],
        compiler_params=_params(("parallel", "arbitrary")),
        name="moe_router",
    )(h, g, shift, scale, router.T)


def _row_copy(a_hbm, xf_ref, sem, src_row, dst_row):
    return pltpu.make_async_copy(a_hbm.at[pl.ds(src_row, 1), :], xf_ref.at[pl.ds(dst_row, 1), :], sem.at[0])


def _ffn_body(rows_ref, next_rows_ref, a_hbm, w1_ref, w3_ref, w2_ref, gate_ref, o_ref, xf_ref, xb_ref, hid_ref,
              sem, *, nt, tf):
    e = pl.program_id(0)
    j = pl.program_id(1)
    n_steps = pl.num_programs(1)
    m = xf_ref.shape[0]
    rows_per_step = m // (nt + xf_ref.shape[1] // tf)

    def wait_all_rows():
        def wait(r, carry):
            _row_copy(a_hbm, xf_ref, sem, 0, r).wait()
            return carry

        lax.fori_loop(0, m, wait, 0, unroll=16)

    def prefetch_next_rows():
        first = j * rows_per_step
        for r in range(rows_per_step):
            _row_copy(a_hbm, xf_ref, sem, next_rows_ref[0, 0, first + r], first + r).start()

    @pl.when((e == 0) & (j == 0))
    def _():
        def start(r, carry):
            _row_copy(a_hbm, xf_ref, sem, rows_ref[0, 0, r], r).start()
            return carry

        lax.fori_loop(0, m, start, 0, unroll=8)

    @pl.when(j == 0)
    def _():
        wait_all_rows()

        def cast(i, carry):
            sl = pl.ds(pl.multiple_of(i * GATHER_CAST_ROWS, GATHER_CAST_ROWS), GATHER_CAST_ROWS)
            xb_ref[sl, :] = xf_ref[sl, :].astype(BF16)
            return carry

        lax.fori_loop(0, m // GATHER_CAST_ROWS, cast, 0)

    @pl.when(j < nt)
    def _():
        prefetch_next_rows()
        x = xb_ref[...]
        h1 = _dot(x, w1_ref[0, 0].astype(BF16))
        h3 = _dot(x, w3_ref[0, 0].astype(BF16))
        hid_ref[j] = (h1 * jax.nn.sigmoid(h1) * h3).astype(BF16)

    @pl.when(j >= nt)
    def _():
        prefetch_next_rows()
        acc = _dot(hid_ref[0], w2_ref[0, 0, 0:tf, :].astype(BF16))
        for k in range(1, nt):
            acc = acc + _dot(hid_ref[k], w2_ref[0, 0, k * tf:(k + 1) * tf, :].astype(BF16))
        o_ref[0] = (acc * gate_ref[0]).astype(o_ref.dtype)

    @pl.when((e == pl.num_programs(0) - 1) & (j == n_steps - 1))
    def _():
        wait_all_rows()


def moe_ffn(rows, a, gate, w1, w3, w2, layer, *, tf):
    e, _, m = rows.shape
    d = a.shape[1]
    f = w1.shape[3]
    nt = f // tf
    nd = d // tf
    assert m % (nt + nd) == 0
    return pl.pallas_call(
        functools.partial(_ffn_body, nt=nt, tf=tf),
        grid=(e, nt + nd),
        in_specs=[pl.BlockSpec((1, 1, m), lambda ei, j: (ei, 0, 0), memory_space=pltpu.SMEM),
                  pl.BlockSpec((1, 1, m), lambda ei, j: ((ei + 1) % e, 0, 0), memory_space=pltpu.SMEM),
                  pl.BlockSpec(memory_space=pl.ANY),
                  pl.BlockSpec((1, 1, d, tf), lambda ei, j: (layer, ei, 0, jnp.minimum(j, nt - 1))),
                  pl.BlockSpec((1, 1, d, tf), lambda ei, j: (layer, ei, 0, jnp.minimum(j, nt - 1))),
                  pl.BlockSpec((1, 1, f, tf), lambda ei, j: (layer, ei, 0, jnp.maximum(j - nt, 0))),
                  pl.BlockSpec((1, m, 1), lambda ei, j: (ei, 0, 0))],
        out_specs=pl.BlockSpec((1, m, tf), lambda ei, j: (ei, 0, jnp.maximum(j - nt, 0))),
        out_shape=jax.ShapeDtypeStruct((e, m, d), BF16),
        scratch_shapes=[pltpu.VMEM((m, d), F32), pltpu.VMEM((m, d), BF16), pltpu.VMEM((nt, m, tf), BF16),
                        pltpu.SemaphoreType.DMA((1,))],
        compiler_params=_params(("arbitrary", "arbitrary")),
        name="moe_ffn",
    )(rows, rows, a, w1, w3, w2, gate)


def prefix_constants(n):
    i = jnp.arange(LANES)
    within = (i[:, None] < i[None, :]).astype(BF16)
    tok_tile = jnp.arange(n) // LANES
    before = (tok_tile[:, None] < i[None, :]).astype(BF16)
    return within, before


def _exclusive_prefix(mask, within_ref, before_ref):
    n = mask.shape[1]
    m = mask.astype(F32).astype(BF16)
    tile_start = _dot(m, before_ref[...])
    parts = []
    for t in range(n // LANES):
        cols = slice(t * LANES, (t + 1) * LANES)
        parts.append(_dot(m[:, cols], within_ref[...]) + tile_start[:, t:t + 1])
    return jnp.concatenate(parts, axis=1), tile_start


def _select_body(aff_ref, within_ref, before_ref, pos_ref, idx_ref, gate_ref, start_ref, pos_all, start_all,
                 *, cap):
    e = pl.program_id(1)
    n = aff_ref.shape[2]

    @pl.when(e == 0)
    def _():
        bits = pltpu.bitcast(aff_ref[0], jnp.int32)

        def refine(i, thr):
            cand = thr | lax.shift_left(jnp.int32(1), 30 - i)
            cnt = jnp.sum((bits >= cand).astype(F32), axis=-1, keepdims=True)
            return jnp.where(cnt >= cap, cand, thr)

        thr = lax.fori_loop(0, 31, refine, jnp.zeros((bits.shape[0], 1), jnp.int32))
        above = bits > thr
        tied = bits == thr
        need = cap - jnp.sum(above.astype(F32), axis=-1, keepdims=True)
        tied_rank, _ = _exclusive_prefix(tied, within_ref, before_ref)
        sel = above | (tied & (tied_rank < need))
        pos, tile_start = _exclusive_prefix(sel, within_ref, before_ref)
        pos_all[...] = jnp.where(sel, pos, -1.0)
        start_all[...] = tile_start

    a = aff_ref[0, pl.ds(e, 1), :]
    pos = pos_all[pl.ds(e, 1), :]
    pos_ref[0, 0] = pos
    start_ref[0, 0] = start_all[pl.ds(e, 1), :]

    slot = lax.broadcasted_iota(jnp.int32, (cap, n), 0).astype(F32)
    onehot = jnp.where(jnp.broadcast_to(pos, (cap, n)) == slot, 1.0, 0.0).astype(BF16)
    tok = lax.broadcasted_iota(jnp.int32, (8, n), 1)
    row = lax.broadcasted_iota(jnp.int32, (8, n), 0)
    a8 = jnp.broadcast_to(a, (8, n))
    a_hi = a8.astype(BF16).astype(F32)
    a_mid = (a8 - a_hi).astype(BF16).astype(F32)
    a_lo = a8 - a_hi - a_mid
    table = jnp.where(row == 0, lax.shift_right_logical(tok, 6).astype(F32),
                      jnp.where(row == 1, (tok & 63).astype(F32),
                                jnp.where(row == 2, a_hi,
                                          jnp.where(row == 3, a_mid, jnp.where(row == 4, a_lo, 0.0)))))
    picked = _dot_nt(table.astype(BF16), onehot)
    idx_ref[0, 0] = (picked[0:1] * 64.0 + picked[1:2]).astype(jnp.int32)
    gate_ref[0, 0] = picked[2:3] + picked[3:4] + picked[4:5]


def moe_select(aff_t, cap):
    b, e, n = aff_t.shape
    within, before = prefix_constants(n)
    row = lambda last: pl.BlockSpec((1, 1, 1, last), lambda bi, ei: (bi, ei, 0, 0))
    return pl.pallas_call(
        functools.partial(_select_body, cap=cap),
        grid=(b, e),
        in_specs=[pl.BlockSpec((1, e, n), lambda bi, ei: (bi, 0, 0)),
                  pl.BlockSpec((LANES, LANES), lambda bi, ei: (0, 0)),
                  pl.BlockSpec((n, LANES), lambda bi, ei: (0, 0))],
        out_specs=[row(n), row(cap), row(cap), row(LANES)],
        out_shape=[jax.ShapeDtypeStruct((b, e, 1, n), F32), jax.ShapeDtypeStruct((b, e, 1, cap), jnp.int32),
                   jax.ShapeDtypeStruct((b, e, 1, cap), F32), jax.ShapeDtypeStruct((b, e, 1, LANES), F32)],
        scratch_shapes=[pltpu.VMEM((e, n), F32), pltpu.VMEM((e, LANES), F32)],
        compiler_params=_params(("parallel", "arbitrary")),
        name="moe_select",
    )(aff_t, within, before)


def _combine_body(start_ref, pos_ref, y_ref, h_ref, gate_ref, fg_ref, o_ref, win_ref, *, tt, final):
    bi = pl.program_id(0)
    ti = pl.program_id(1)
    n_exp, _, cap, d = y_ref.shape
    w = COMBINE_WINDOW

    los = [start_ref[bi, e, ti] for e in range(n_exp)]
    his = [start_ref[bi, e, ti + 1] for e in range(n_exp)]
    bases = [(lo // BF16_ROWS) * BF16_ROWS for lo in los]
    n_pass = functools.reduce(jnp.maximum, [(hi - base + w - 1) // w for hi, base in zip(his, bases)])
    lane = lax.broadcasted_iota(jnp.int32, (1, 2 * w), 1)
    left = lane < w

    def window_sum(p):
        hots = []
        for e in range(0, n_exp, 2):
            wants, starts = [], []
            for k in range(2):
                want = bases[e + k] + p * w
                st = pl.multiple_of(jnp.minimum(want, cap - w), BF16_ROWS)
                win_ref[(e + k) * w:(e + k + 1) * w, :] = y_ref[e + k, 0, pl.ds(st, w), :]
                wants.append(want)
                starts.append(st)
            slot = jnp.where(left, starts[0] + lane, starts[1] + lane - w)
            slot = jnp.where(slot >= jnp.where(left, wants[0], wants[1]), slot, -2).astype(F32)
            tok_pos = jnp.where(left, pos_ref[0, :, e:e + 1], pos_ref[0, :, e + 1:e + 2])
            hots.append(jnp.where(tok_pos == slot, 1.0, 0.0).astype(BF16))
        return _dot(jnp.concatenate(hots, axis=1), win_ref[...])

    acc = lax.fori_loop(1, n_pass, lambda p, a: a + window_sum(p), window_sum(0))
    h = h_ref[0] + gate_ref[0] * acc
    if final:
        h = h * lax.rsqrt(jnp.mean(h * h, axis=-1, keepdims=True) + EPS) * fg_ref[...]
    o_ref[0] = h


def moe_combine(tile_start, pos_t, y, h, gate, final_g, *, tt, final):
    b, n, d = h.shape
    e, _, cap, _ = y.shape
    grid_spec = pltpu.PrefetchScalarGridSpec(
        num_scalar_prefetch=1,
        grid=(b, n // tt),
        in_specs=[pl.BlockSpec((1, tt, e), lambda bi, i, s: (bi, i, 0)),
                  pl.BlockSpec((e, 1, cap, d), lambda bi, i, s: (0, bi, 0, 0), pipeline_mode=pl.Buffered(1)),
                  pl.BlockSpec((1, tt, d), lambda bi, i, s: (bi, i, 0)),
                  pl.BlockSpec((1, 1, d), lambda bi, i, s: (bi, 0, 0)),
                  pl.BlockSpec((1, d), lambda bi, i, s: (0, 0))],
        out_specs=pl.BlockSpec((1, tt, d), lambda bi, i, s: (bi, i, 0)),
        scratch_shapes=[pltpu.VMEM((e * COMBINE_WINDOW, d), BF16)])
    return pl.pallas_call(
        functools.partial(_combine_body, tt=tt, final=final),
        grid_spec=grid_spec,
        out_shape=jax.ShapeDtypeStruct((b, n, d), F32),
        compiler_params=_params(("arbitrary", "arbitrary")),
        name="moe_combine",
    )(tile_start, pos_t, y, h, gate, final_g)


def _expert_choice_moe(h, g, shift, scale, gate2, router, w1, w3, w2, layer, final_g, *, final):
    b, n, d = h.shape
    e = router.shape[1]
    cap = CAPACITY_FACTOR * n // e
    a, aff_t = moe_router(h, g, shift, scale, router, tm=512)
    pos, idx, gate, start = moe_select(aff_t, cap)
    rows = idx[:, :, 0, :] + (jnp.arange(b, dtype=jnp.int32) * n)[:, None, None]
    rows = rows.transpose(1, 0, 2).reshape(e, 1, b * cap)
    gate_e = gate[:, :, 0, :].transpose(1, 0, 2).reshape(e, b * cap, 1)
    y = moe_ffn(rows, a.reshape(b * n, d), gate_e, w1, w3, w2, layer, tf=FFN_TILE)
    stride = COMBINE_TOKENS // LANES
    tile_start = start[:, :, 0, 0:n // LANES + 1:stride].astype(jnp.int32)
    pos_t = pos[:, :, 0, :].transpose(0, 2, 1)
    return moe_combine(tile_start, pos_t, y.reshape(e, b, cap, d), h, gate2, final_g,
                       tt=COMBINE_TOKENS, final=final)


def kernel(x, c, ctx, c_ctx, mod_w, mod_b, norm1_g, norm2_g, router, w1, w3, w2, attn_w_in, attn_w_out,
           na_rpb, q_norm_g, k_norm_g, sgu_w_in, sgu_w_out, sgu_ws, sgu_b, sgu_ln_g, sgu_ln_b, final_norm_g):
    b, n, d = x.shape
    cvec = jnp.concatenate([c, c_ctx[None], jnp.zeros((8 - b - 1, d), c.dtype)], axis=0)
    mods = adaln(cvec, mod_w, mod_b)
    cos, sin = rope_tables(n)
    h = x
    for l in range(DEPTH):
        mod = [mods[l, :, k * d:(k + 1) * d] for k in range(N_MOD)]
        sh1, sc1, g1, sh2, sc2, g2 = (m[:b, None, :] for m in mod)
        n1 = norm1_g[l].reshape(1, d)
        if l % 2 == 0:
            ev = l // 2
            csh1, csc1 = (jnp.broadcast_to(m[b][None, None, :], (b, 1, d)) for m in mod[:2])
            w_in = attn_w_in[ev].astype(BF16)
            proj = norm_mod_matmul(h, n1, sh1, sc1, w_in)
            ctxp = norm_mod_matmul(ctx, n1, csh1, csc1, w_in[:, Q_COLS:])
            tl, tr = na_bias_tables(na_rpb[ev])
            heads_a = neighbourhood_attention(proj, ctxp, tl, tr)
            heads_b = gqa_attention(proj, ctxp, q_norm_g[ev].reshape(1, -1), k_norm_g[ev].reshape(1, -1),
                                    cos, sin, tq=GQA_Q_TILE)
            w_out = attn_w_out[ev].astype(BF16)
            h = matmul_residual([(heads_a, w_out[:NA_W]), (heads_b, w_out[NA_W:])], h, g1)
        else:
            o = l // 2
            z = norm_mod_matmul(h, n1, sh1, sc1, sgu_w_in[o].astype(BF16), gelu=True)
            mixed = sgu_mix(z, sgu_ws[o], sgu_b[o], sgu_ln_g[o], sgu_ln_b[o])
            h = matmul_residual([(mixed, sgu_w_out[o].astype(BF16))], h, g1)
        h = _expert_choice_moe(h, norm2_g[l].reshape(1, d), sh2, sc2, g2, router[l], w1, w3, w2, l,
                               final_norm_g.reshape(1, d), final=(l == DEPTH - 1))
    return h
```

```python
import functools

import jax
import jax.numpy as jnp
from jax import lax
from jax.experimental import pallas as pl
from jax.experimental.pallas import tpu as pltpu

D_MODEL = 2048
DEPTH = 2
GRID_W = 64
CTX_LEN = 256
HEAD_DIM = 128
ATTN_SCALE = HEAD_DIM ** -0.5
NA_HEADS = 8
NA_KH = 8
NA_KW = 16
GQA_Q_HEADS = 8
GQA_KV_HEADS = 2
GQA_GROUP = GQA_Q_HEADS // GQA_KV_HEADS
ROPE_THETA = 10000.0
CHUNK = 128
SGU_GROUPS = 8
SGU_WIDTH = 2 * D_MODEL
N_EXPERTS = 16
CAPACITY_FACTOR = 2
N_MOD = 6
EPS = 1e-6
NA_W = NA_HEADS * HEAD_DIM
Q_COLS = NA_W + GQA_Q_HEADS * HEAD_DIM

QA_BLK, QB_BLK, KA_BLK, VA_BLK, KB_BLK, VB_BLK = 0, 8, 16, 24, 32, 34
CKA_BLK, CVA_BLK, CKB_BLK, CVB_BLK = 0, 8, 16, 18

NEG_INF = -1e30
V7X_VMEM_LIMIT_BYTES = 56 * 1024 * 1024
BF16 = jnp.bfloat16
F32 = jnp.float32

NA_Q_ROWS = 4
NA_K_ROWS = 12
NA_MASKED = 2 * NA_KH - 1
LOG2_E = 1.4426950408889634
NA_HEADS_PER_STEP = 4

GQA_KEY_CHUNK = 512
GQA_Q_CHUNK = 256
GQA_Q_TILE = 2048
LANES = 128
BF16_ROWS = 16
FFN_TILE = 256
GATHER_CAST_ROWS = 256
COMBINE_TOKENS = 256
COMBINE_WINDOW = 64
SGU_CHUNKS_PER_STEP = 4
MATMUL_ROWS = 1024
MATMUL_MAX_COLS = 1536
RESIDUAL_COLS = (512, 1024)
RESIDUAL_VMEM_BUDGET = 40 * 1024 * 1024
MXU_COLS = 256


def _col_tile(n_out):
    return max(t for t in range(MXU_COLS, MATMUL_MAX_COLS + 1, MXU_COLS) if n_out % t == 0)


def _params(semantics):
    return pltpu.CompilerParams(dimension_semantics=semantics,
                                vmem_limit_bytes=V7X_VMEM_LIMIT_BYTES)


def _dot(a, b):
    return jnp.dot(a, b, preferred_element_type=F32)


def _dot_nt(a, b):
    return lax.dot_general(a, b, (((1,), (1,)), ((), ())), preferred_element_type=F32)


def _adaln_body(c_ref, w_ref, b_ref, o_ref):
    c = c_ref[...]
    s = (c * jax.nn.sigmoid(c)).astype(BF16)
    o_ref[0] = _dot(s, w_ref[0].astype(BF16)) + b_ref[0]


def adaln(cvec, mod_w, mod_b, tn=1024):
    L, d, n = mod_w.shape
    return pl.pallas_call(
        _adaln_body,
        grid=(L, n // tn),
        in_specs=[pl.BlockSpec((8, d), lambda l, j: (0, 0)),
                  pl.BlockSpec((1, d, tn), lambda l, j: (l, 0, j)),
                  pl.BlockSpec((1, 1, tn), lambda l, j: (l, 0, j))],
        out_specs=pl.BlockSpec((1, 8, tn), lambda l, j: (l, 0, j)),
        out_shape=jax.ShapeDtypeStruct((L, 8, n), F32),
        compiler_params=_params(("parallel", "arbitrary")),
        name="adaln",
    )(cvec, mod_w, mod_b.reshape(L, 1, n))


def _modulated(x, g, sh, sc):
    y = x * lax.rsqrt(jnp.mean(x * x, axis=-1, keepdims=True) + EPS) * g
    return y * (1 + sc) + sh


def _nmm_body(x_ref, g_ref, sh_ref, sc_ref, w_ref, o_ref, a_ref, *, gelu):
    @pl.when(pl.program_id(2) == 0)
    def _():
        a_ref[...] = _modulated(x_ref[0], g_ref[...], sh_ref[0], sc_ref[0]).astype(BF16)

    z = _dot(a_ref[...], w_ref[...])
    if gelu:
        z = jax.nn.gelu(z)
    o_ref[0] = z.astype(o_ref.dtype)


def norm_mod_matmul(h, g, shift, scale, w, *, gelu=False):
    b, n, d = h.shape
    n_out = w.shape[1]
    tm = min(n, MATMUL_ROWS)
    tn = _col_tile(n_out)
    return pl.pallas_call(
        functools.partial(_nmm_body, gelu=gelu),
        grid=(b, n // tm, n_out // tn),
        in_specs=[pl.BlockSpec((1, tm, d), lambda bi, i, j: (bi, i, 0)),
                  pl.BlockSpec((1, d), lambda bi, i, j: (0, 0)),
                  pl.BlockSpec((1, 1, d), lambda bi, i, j: (bi, 0, 0)),
                  pl.BlockSpec((1, 1, d), lambda bi, i, j: (bi, 0, 0)),
                  pl.BlockSpec((d, tn), lambda bi, i, j: (0, j))],
        out_specs=pl.BlockSpec((1, tm, tn), lambda bi, i, j: (bi, i, j)),
        out_shape=jax.ShapeDtypeStruct((b, n, n_out), BF16),
        scratch_shapes=[pltpu.VMEM((tm, d), BF16)],
        compiler_params=_params(("parallel", "parallel", "arbitrary")),
        name="norm_mod_matmul",
    )(h, g, shift, scale, w)


def _mmr_body(*refs, n_parts):
    a_refs, w_refs = refs[:n_parts], refs[n_parts:2 * n_parts]
    h_ref, gate_ref, o_ref = refs[2 * n_parts:]
    acc = _dot(a_refs[0][0], w_refs[0][...])
    for a_ref, w_ref in zip(a_refs[1:], w_refs[1:]):
        acc = acc + _dot(a_ref[0], w_ref[...])
    o_ref[0] = h_ref[0] + gate_ref[0] * acc


def matmul_residual(parts, h, gate):
    b, n, d = h.shape
    tm = min(n, MATMUL_ROWS)
    k_total = sum(w.shape[0] for _, w in parts)

    def step_bytes(tn):
        return 2 * (2 * tm * k_total + 2 * k_total * tn + 4 * tm * tn + 4 * tm * tn)

    tn = max(t for t in RESIDUAL_COLS if d % t == 0 and step_bytes(t) <= RESIDUAL_VMEM_BUDGET)
    a_list = [a for a, _ in parts]
    w_list = [w for _, w in parts]
    return pl.pallas_call(
        functools.partial(_mmr_body, n_parts=len(parts)),
        grid=(b, n // tm, d // tn),
        in_specs=([pl.BlockSpec((1, tm, a.shape[2]), lambda bi, i, j: (bi, i, 0)) for a in a_list]
                  + [pl.BlockSpec((w.shape[0], tn), lambda bi, i, j: (0, j)) for w in w_list]
                  + [pl.BlockSpec((1, tm, tn), lambda bi, i, j: (bi, i, j)),
                     pl.BlockSpec((1, 1, tn), lambda bi, i, j: (bi, 0, j))]),
        out_specs=pl.BlockSpec((1, tm, tn), lambda bi, i, j: (bi, i, j)),
        out_shape=jax.ShapeDtypeStruct((b, n, d), F32),
        compiler_params=_params(("parallel", "parallel", "arbitrary")),
        name="matmul_residual",
    )(*a_list, *w_list, h, gate)


def na_bias_tables(rpb):
    cols = jnp.arange(GRID_W)
    start = jnp.clip(cols - NA_KW // 2, 0, GRID_W - NA_KW)
    j = jnp.arange(GRID_W)
    inside = (j[None, :] >= start[:, None]) & (j[None, :] < start[:, None] + NA_KW)
    dcol = jnp.clip(j[None, :] - cols[:, None] + NA_KW - 1, 0, 2 * NA_KW - 2)
    t = jnp.where(inside[None, None], rpb[:, :, dcol] * LOG2_E, NEG_INF)
    t = jnp.concatenate([t, jnp.full_like(t[:, :1], NEG_INF)], axis=1)
    z = jnp.zeros_like(t)
    return jnp.concatenate([t, z], axis=-1), jnp.concatenate([z, t], axis=-1)


def _na_body(q_ref, k_ref, v_ref, kc_ref, vc_ref, tl_ref, tr_ref, o_ref, s_ref):
    rows = k_ref.shape[1] // GRID_W
    r0 = pl.program_id(2) * NA_Q_ROWS
    k0 = jnp.clip(r0 - NA_KH // 2, 0, rows - NA_K_ROWS)
    kstart = pl.multiple_of(k0 * GRID_W, GRID_W)
    keys = pl.ds(kstart, NA_K_ROWS * GRID_W)
    scale = ATTN_SCALE * LOG2_E

    def table_index(kr, r, rs):
        valid = (kr >= rs) & (kr < rs + NA_KH)
        return jnp.where(valid, kr - r + NA_KH - 1, NA_MASKED)

    table_ids = []
    for qr in range(NA_Q_ROWS):
        r = r0 + qr
        rs = jnp.clip(r - NA_KH // 2, 0, rows - NA_KH)
        table_ids.append([(table_index(k0 + 2 * p, r, rs), table_index(k0 + 2 * p + 1, r, rs))
                          for p in range(NA_K_ROWS // 2)])

    for hh in range(NA_HEADS_PER_STEP):
        lanes = slice(hh * HEAD_DIM, (hh + 1) * HEAD_DIM)
        q = q_ref[0, :, lanes]
        s_ref[hh] = _dot_nt(q, k_ref[0, keys, lanes]) * scale
        s_ctx = _dot_nt(q, kc_ref[0, :, lanes]) * scale
        for qr in range(NA_Q_ROWS):
            for p in range(NA_K_ROWS // 2):
                ia, ib = table_ids[qr][p]
                blk = (hh, slice(qr * GRID_W, (qr + 1) * GRID_W), slice(p * 2 * GRID_W, (p + 1) * 2 * GRID_W))
                s_ref[blk] = s_ref[blk] + tl_ref[hh, ia] + tr_ref[hh, ib]
        s_loc = s_ref[hh]
        m = jnp.maximum(jnp.max(s_loc, axis=-1, keepdims=True), jnp.max(s_ctx, axis=-1, keepdims=True))
        p_loc = jnp.exp2(s_loc - m)
        p_ctx = jnp.exp2(s_ctx - m)
        denom = jnp.sum(p_loc, axis=-1, keepdims=True) + jnp.sum(p_ctx, axis=-1, keepdims=True)
        o = _dot(p_loc.astype(BF16), v_ref[0, keys, lanes]) + _dot(p_ctx.astype(BF16), vc_ref[0, :, lanes])
        o_ref[0, :, lanes] = (o / denom).astype(o_ref.dtype)


def neighbourhood_attention(proj, ctxp, tl, tr):
    b, n, _ = proj.shape
    lc = ctxp.shape[1]
    tq = NA_Q_ROWS * GRID_W
    hs = NA_HEADS_PER_STEP
    w = hs * HEAD_DIM
    table = pl.BlockSpec((hs, 2 * NA_KH, GRID_W, 2 * GRID_W), lambda bi, h, t: (h, 0, 0, 0))
    return pl.pallas_call(
        _na_body,
        grid=(b, NA_HEADS // hs, n // tq),
        in_specs=[pl.BlockSpec((1, tq, w), lambda bi, h, t: (bi, t, QA_BLK // hs + h)),
                  pl.BlockSpec((1, n, w), lambda bi, h, t: (bi, 0, KA_BLK // hs + h)),
                  pl.BlockSpec((1, n, w), lambda bi, h, t: (bi, 0, VA_BLK // hs + h)),
                  pl.BlockSpec((1, lc, w), lambda bi, h, t: (bi, 0, CKA_BLK // hs + h)),
                  pl.BlockSpec((1, lc, w), lambda bi, h, t: (bi, 0, CVA_BLK // hs + h)),
                  table, table],
        out_specs=pl.BlockSpec((1, tq, w), lambda bi, h, t: (bi, t, h)),
        out_shape=jax.ShapeDtypeStruct((b, n, NA_W), BF16),
        scratch_shapes=[pltpu.VMEM((hs, tq, NA_K_ROWS * GRID_W), F32)],
        compiler_params=_params(("parallel", "parallel", "arbitrary")),
        name="neighbourhood_attention",
    )(proj, proj, proj, ctxp, ctxp, tl, tr)


def rope_tables(n):
    t = jnp.arange(n)
    row = (t // GRID_W).astype(F32)
    col = (t % GRID_W).astype(F32)
    axis_dims = HEAD_DIM // 2
    inv_freq = ROPE_THETA ** (-jnp.arange(0, axis_dims, 2, dtype=F32) / axis_dims)
    ang_r = row[:, None] * inv_freq
    ang_c = col[:, None] * inv_freq
    cos = jnp.concatenate([jnp.cos(ang_r), jnp.cos(ang_r), jnp.cos(ang_c), jnp.cos(ang_c)], axis=-1)
    sin = jnp.concatenate([-jnp.sin(ang_r), jnp.sin(ang_r), -jnp.sin(ang_c), jnp.sin(ang_c)], axis=-1)
    return cos, sin


def _head_rms(x, g):
    return x * lax.rsqrt(jnp.mean(x * x, axis=-1, keepdims=True) + EPS) * g


def _rope(x, cos, sin):
    quarter = HEAD_DIM // 4
    lane = lax.broadcasted_iota(jnp.int32, x.shape, 1)
    partner = jnp.where(lane % (2 * quarter) < quarter,
                        pltpu.roll(x, HEAD_DIM - quarter, 1), pltpu.roll(x, quarter, 1))
    return x * cos + partner * sin


def _gqa_body(q_ref, k_ref, v_ref, kc_ref, vc_ref, qg_ref, kg_ref, cq_ref, sq_ref, ck_ref, sk_ref,
              o_ref, kt_ref):
    n = k_ref.shape[1]
    lc = kc_ref.shape[1]

    @pl.when(pl.program_id(2) == 0)
    def _():
        kg = kg_ref[...]
        for c in range(n // GQA_KEY_CHUNK):
            rows = slice(c * GQA_KEY_CHUNK, (c + 1) * GQA_KEY_CHUNK)
            kn = _rope(_head_rms(k_ref[0, rows, :].astype(F32), kg), ck_ref[rows, :], sk_ref[rows, :])
            kt_ref[:, rows] = kn.T.astype(BF16)
        kt_ref[:, n:n + lc] = _head_rms(kc_ref[0].astype(F32), kg).T.astype(BF16)

    for c in range(q_ref.shape[1] // GQA_Q_CHUNK):
        rows = slice(c * GQA_Q_CHUNK, (c + 1) * GQA_Q_CHUNK)
        q = _rope(_head_rms(q_ref[0, rows, :].astype(F32), qg_ref[...]), cq_ref[rows, :], sq_ref[rows, :])
        q = (q * (ATTN_SCALE * LOG2_E)).astype(BF16)
        s = _dot(q, kt_ref[...])
        p = jnp.exp2(s - jnp.max(s, axis=-1, keepdims=True))
        denom = jnp.sum(p, axis=-1, keepdims=True)
        pb = p.astype(BF16)
        o = _dot(pb[:, :n], v_ref[0]) + _dot(pb[:, n:], vc_ref[0])
        o_ref[0, rows, :] = (o / denom).astype(o_ref.dtype)


def gqa_attention(proj, ctxp, qn_g, kn_g, cos, sin, *, tq):
    b, n, _ = proj.shape
    lc = ctxp.shape[1]
    hd = HEAD_DIM
    nq = n // tq

    def qhead(kv, i):
        return kv * GQA_GROUP + i // nq

    return pl.pallas_call(
        _gqa_body,
        grid=(b, GQA_KV_HEADS, GQA_GROUP * nq),
        in_specs=[pl.BlockSpec((1, tq, hd), lambda bi, kv, i: (bi, i % nq, QB_BLK + qhead(kv, i))),
                  pl.BlockSpec((1, n, hd), lambda bi, kv, i: (bi, 0, KB_BLK + kv)),
                  pl.BlockSpec((1, n, hd), lambda bi, kv, i: (bi, 0, VB_BLK + kv)),
                  pl.BlockSpec((1, lc, hd), lambda bi, kv, i: (bi, 0, CKB_BLK + kv)),
                  pl.BlockSpec((1, lc, hd), lambda bi, kv, i: (bi, 0, CVB_BLK + kv)),
                  pl.BlockSpec((1, hd), lambda bi, kv, i: (0, 0)),
                  pl.BlockSpec((1, hd), lambda bi, kv, i: (0, 0)),
                  pl.BlockSpec((tq, hd), lambda bi, kv, i: (i % nq, 0)),
                  pl.BlockSpec((tq, hd), lambda bi, kv, i: (i % nq, 0)),
                  pl.BlockSpec((n, hd), lambda bi, kv, i: (0, 0)),
                  pl.BlockSpec((n, hd), lambda bi, kv, i: (0, 0))],
        out_specs=pl.BlockSpec((1, tq, hd), lambda bi, kv, i: (bi, i % nq, qhead(kv, i))),
        out_shape=jax.ShapeDtypeStruct((b, n, GQA_Q_HEADS * hd), BF16),
        scratch_shapes=[pltpu.VMEM((hd, n + lc), BF16)],
        compiler_params=_params(("parallel", "parallel", "arbitrary")),
        name="gqa_attention",
    )(proj, proj, proj, ctxp, ctxp, qn_g, kn_g, cos, sin, cos, sin)


def _sgu_body(z_ref, ws_ref, bs_ref, g_ref, b_ref, o_ref):
    dg = SGU_WIDTH // SGU_GROUPS
    for c in range(z_ref.shape[1] // CHUNK):
        rows = slice(c * CHUNK, (c + 1) * CHUNK)
        v = z_ref[0, rows, SGU_WIDTH:].astype(F32)
        mu = jnp.mean(v, axis=-1, keepdims=True)
        vc = v - mu
        var = jnp.mean(vc * vc, axis=-1, keepdims=True)
        vn = (vc * lax.rsqrt(var + EPS) * g_ref[...] + b_ref[...]).astype(BF16)
        for g in range(SGU_GROUPS):
            cols = slice(g * dg, (g + 1) * dg)
            mixed = _dot(ws_ref[g].astype(BF16), vn[:, cols]) + bs_ref[:, g:g + 1]
            o_ref[0, rows, cols] = (z_ref[0, rows, cols].astype(F32) * mixed).astype(o_ref.dtype)


def sgu_mix(z, ws, bs, ln_g, ln_b):
    b, n, _ = z.shape
    rows = SGU_CHUNKS_PER_STEP * CHUNK
    return pl.pallas_call(
        _sgu_body,
        grid=(b, n // rows),
        in_specs=[pl.BlockSpec((1, rows, 2 * SGU_WIDTH), lambda bi, c: (bi, c, 0)),
                  pl.BlockSpec((SGU_GROUPS, CHUNK, CHUNK), lambda bi, c: (0, 0, 0)),
                  pl.BlockSpec((CHUNK, SGU_GROUPS), lambda bi, c: (0, 0)),
                  pl.BlockSpec((1, SGU_WIDTH), lambda bi, c: (0, 0)),
                  pl.BlockSpec((1, SGU_WIDTH), lambda bi, c: (0, 0))],
        out_specs=pl.BlockSpec((1, rows, SGU_WIDTH), lambda bi, c: (bi, c, 0)),
        out_shape=jax.ShapeDtypeStruct((b, n, SGU_WIDTH), BF16),
        compiler_params=_params(("parallel", "arbitrary")),
        name="sgu_mix",
    )(z, ws, bs.T, ln_g.reshape(1, -1), ln_b.reshape(1, -1))


def _split_bf16(x):
    hi = x.astype(BF16)
    return hi, (x - hi.astype(F32)).astype(BF16)


def _router_body(x_ref, g_ref, sh_ref, sc_ref, rt_ref, a_ref, aff_ref):
    a = _modulated(x_ref[0], g_ref[...], sh_ref[0], sc_ref[0])
    a_ref[0] = a
    a_hi, a_lo = _split_bf16(a)
    r_hi, r_lo = _split_bf16(rt_ref[...])
    logits = _dot_nt(r_hi, a_hi) + (_dot_nt(r_hi, a_lo) + _dot_nt(r_lo, a_hi))
    e = jnp.exp(logits - jnp.max(logits, axis=0, keepdims=True))
    aff_ref[0] = e / jnp.sum(e, axis=0, keepdims=True)


def moe_router(h, g, shift, scale, router, *, tm):
    b, n, d = h.shape
    e = router.shape[1]
    return pl.pallas_call(
        _router_body,
        grid=(b, n // tm),
        in_specs=[pl.BlockSpec((1, tm, d), lambda bi, i: (bi, i, 0)),
                  pl.BlockSpec((1, d), lambda bi, i: (0, 0)),
                  pl.BlockSpec((1, 1, d), lambda bi, i: (bi, 0, 0)),
                  pl.BlockSpec((1, 1, d), lambda bi, i: (bi, 0, 0)),
                  pl.BlockSpec((e, d), lambda bi, i: (0, 0))],
        out_specs=[pl.BlockSpec((1, tm, d), lambda bi, i: (bi, i, 0)),
                   pl.BlockSpec((1, e, tm), lambda bi, i: (bi, 0, i))],
        out_shape=[jax.ShapeDtypeStruct((b, n, d), F32), jax.ShapeDtypeStruct((b, e, n), F32)],
        compiler_params=_params(("parallel", "arbitrary")),
        name="moe_router",
    )(h, g, shift, scale, router.T)


def _row_copy(a_hbm, xf_ref, sem, src_row, dst_row):
    return pltpu.make_async_copy(a_hbm.at[pl.ds(src_row, 1), :], xf_ref.at[pl.ds(dst_row, 1), :], sem.at[0])


def _ffn_body(rows_ref, next_rows_ref, a_hbm, w1_ref, w3_ref, w2_ref, gate_ref, o_ref, xf_ref, xb_ref, hid_ref,
              sem, *, nt, tf):
    e = pl.program_id(0)
    j = pl.program_id(1)
    n_steps = pl.num_programs(1)
    m = xf_ref.shape[0]
    rows_per_step = m // (nt + xf_ref.shape[1] // tf)

    def wait_all_rows():
        def wait(r, carry):
            _row_copy(a_hbm, xf_ref, sem, 0, r).wait()
            return carry

        lax.fori_loop(0, m, wait, 0, unroll=16)

    def prefetch_next_rows():
        first = j * rows_per_step
        for r in range(rows_per_step):
            _row_copy(a_hbm, xf_ref, sem, next_rows_ref[0, 0, first + r], first + r).start()

    @pl.when((e == 0) & (j == 0))
    def _():
        def start(r, carry):
            _row_copy(a_hbm, xf_ref, sem, rows_ref[0, 0, r], r).start()
            return carry

        lax.fori_loop(0, m, start, 0, unroll=8)

    @pl.when(j == 0)
    def _():
        wait_all_rows()

        def cast(i, carry):
            sl = pl.ds(pl.multiple_of(i * GATHER_CAST_ROWS, GATHER_CAST_ROWS), GATHER_CAST_ROWS)
            xb_ref[sl, :] = xf_ref[sl, :].astype(BF16)
            return carry

        lax.fori_loop(0, m // GATHER_CAST_ROWS, cast, 0)

    @pl.when(j < nt)
    def _():
        prefetch_next_rows()
        x = xb_ref[...]
        h1 = _dot(x, w1_ref[0, 0].astype(BF16))
        h3 = _dot(x, w3_ref[0, 0].astype(BF16))
        hid_ref[j] = (h1 * jax.nn.sigmoid(h1) * h3).astype(BF16)

    @pl.when(j >= nt)
    def _():
        prefetch_next_rows()
        acc = _dot(hid_ref[0], w2_ref[0, 0, 0:tf, :].astype(BF16))
        for k in range(1, nt):
            acc = acc + _dot(hid_ref[k], w2_ref[0, 0, k * tf:(k + 1) * tf, :].astype(BF16))
        o_ref[0] = (acc * gate_ref[0]).astype(o_ref.dtype)

    @pl.when((e == pl.num_programs(0) - 1) & (j == n_steps - 1))
    def _():
        wait_all_rows()


def moe_ffn(rows, a, gate, w1, w3, w2, layer, *, tf):
    e, _, m = rows.shape
    d = a.shape[1]
    f = w1.shape[3]
    nt = f // tf
    nd = d // tf
    assert m % (nt + nd) == 0
    return pl.pallas_call(
        functools.partial(_ffn_body, nt=nt, tf=tf),
        grid=(e, nt + nd),
        in_specs=[pl.BlockSpec((1, 1, m), lambda ei, j: (ei, 0, 0), memory_space=pltpu.SMEM),
                  pl.BlockSpec((1, 1, m), lambda ei, j: ((ei + 1) % e, 0, 0), memory_space=pltpu.SMEM),
                  pl.BlockSpec(memory_space=pl.ANY),
                  pl.BlockSpec((1, 1, d, tf), lambda ei, j: (layer, ei, 0, jnp.minimum(j, nt - 1))),
                  pl.BlockSpec((1, 1, d, tf), lambda ei, j: (layer, ei, 0, jnp.minimum(j, nt - 1))),
                  pl.BlockSpec((1, 1, f, tf), lambda ei, j: (layer, ei, 0, jnp.maximum(j - nt, 0))),
                  pl.BlockSpec((1, m, 1), lambda ei, j: (ei, 0, 0))],
        out_specs=pl.BlockSpec((1, m, tf), lambda ei, j: (ei, 0, jnp.maximum(j - nt, 0))),
        out_shape=jax.ShapeDtypeStruct((e, m, d), BF16),
        scratch_shapes=[pltpu.VMEM((m, d), F32), pltpu.VMEM((m, d), BF16), pltpu.VMEM((nt, m, tf), BF16),
                        pltpu.SemaphoreType.DMA((1,))],
        compiler_params=_params(("arbitrary", "arbitrary")),
        name="moe_ffn",
    )(rows, rows, a, w1, w3, w2, gate)


def prefix_constants(n):
    i = jnp.arange(LANES)
    within = (i[:, None] < i[None, :]).astype(BF16)
    tok_tile = jnp.arange(n) // LANES
    before = (tok_tile[:, None] < i[None, :]).astype(BF16)
    return within, before


def _exclusive_prefix(mask, within_ref, before_ref):
    n = mask.shape[1]
    m = mask.astype(F32).astype(BF16)
    tile_start = _dot(m, before_ref[...])
    parts = []
    for t in range(n // LANES):
        cols = slice(t * LANES, (t + 1) * LANES)
        parts.append(_dot(m[:, cols], within_ref[...]) + tile_start[:, t:t + 1])
    return jnp.concatenate(parts, axis=1), tile_start


def _select_body(aff_ref, within_ref, before_ref, pos_ref, idx_ref, gate_ref, start_ref, pos_all, start_all,
                 *, cap):
    e = pl.program_id(1)
    n = aff_ref.shape[2]

    @pl.when(e == 0)
    def _():
        bits = pltpu.bitcast(aff_ref[0], jnp.int32)

        def refine(i, thr):
            cand = thr | lax.shift_left(jnp.int32(1), 30 - i)
            cnt = jnp.sum((bits >= cand).astype(F32), axis=-1, keepdims=True)
            return jnp.where(cnt >= cap, cand, thr)

        thr = lax.fori_loop(0, 31, refine, jnp.zeros((bits.shape[0], 1), jnp.int32))
        above = bits > thr
        tied = bits == thr
        need = cap - jnp.sum(above.astype(F32), axis=-1, keepdims=True)
        tied_rank, _ = _exclusive_prefix(tied, within_ref, before_ref)
        sel = above | (tied & (tied_rank < need))
        pos, tile_start = _exclusive_prefix(sel, within_ref, before_ref)
        pos_all[...] = jnp.where(sel, pos, -1.0)
        start_all[...] = tile_start

    a = aff_ref[0, pl.ds(e, 1), :]
    pos = pos_all[pl.ds(e, 1), :]
    pos_ref[0, 0] = pos
    start_ref[0, 0] = start_all[pl.ds(e, 1), :]

    slot = lax.broadcasted_iota(jnp.int32, (cap, n), 0).astype(F32)
    onehot = jnp.where(jnp.broadcast_to(pos, (cap, n)) == slot, 1.0, 0.0).astype(BF16)
    tok = lax.broadcasted_iota(jnp.int32, (8, n), 1)
    row = lax.broadcasted_iota(jnp.int32, (8, n), 0)
    a8 = jnp.broadcast_to(a, (8, n))
    a_hi = a8.astype(BF16).astype(F32)
    a_mid = (a8 - a_hi).astype(BF16).astype(F32)
    a_lo = a8 - a_hi - a_mid
    table = jnp.where(row == 0, lax.shift_right_logical(tok, 6).astype(F32),
                      jnp.where(row == 1, (tok & 63).astype(F32),
                                jnp.where(row == 2, a_hi,
                                          jnp.where(row == 3, a_mid, jnp.where(row == 4, a_lo, 0.0)))))
    picked = _dot_nt(table.astype(BF16), onehot)
    idx_ref[0, 0] = (picked[0:1] * 64.0 + picked[1:2]).astype(jnp.int32)
    gate_ref[0, 0] = picked[2:3] + picked[3:4] + picked[4:5]


def moe_select(aff_t, cap):
    b, e, n = aff_t.shape
    within, before = prefix_constants(n)
    row = lambda last: pl.BlockSpec((1, 1, 1, last), lambda bi, ei: (bi, ei, 0, 0))
    return pl.pallas_call(
        functools.partial(_select_body, cap=cap),
        grid=(b, e),
        in_specs=[pl.BlockSpec((1, e, n), lambda bi, ei: (bi, 0, 0)),
                  pl.BlockSpec((LANES, LANES), lambda bi, ei: (0, 0)),
                  pl.BlockSpec((n, LANES), lambda bi, ei: (0, 0))],
        out_specs=[row(n), row(cap), row(cap), row(LANES)],
        out_shape=[jax.ShapeDtypeStruct((b, e, 1, n), F32), jax.ShapeDtypeStruct((b, e, 1, cap), jnp.int32),
                   jax.ShapeDtypeStruct((b, e, 1, cap), F32), jax.ShapeDtypeStruct((b, e, 1, LANES), F32)],
        scratch_shapes=[pltpu.VMEM((e, n), F32), pltpu.VMEM((e, LANES), F32)],
        compiler_params=_params(("parallel", "arbitrary")),
        name="moe_select",
    )(aff_t, within, before)


def _combine_body(start_ref, pos_ref, y_ref, h_ref, gate_ref, fg_ref, o_ref, win_ref, *, tt, final):
    bi = pl.program_id(0)
    ti = pl.program_id(1)
    n_exp, _, cap, d = y_ref.shape
    w = COMBINE_WINDOW

    los = [start_ref[bi, e, ti] for e in range(n_exp)]
    his = [start_ref[bi, e, ti + 1] for e in range(n_exp)]
    bases = [(lo // BF16_ROWS) * BF16_ROWS for lo in los]
    n_pass = functools.reduce(jnp.maximum, [(hi - base + w - 1) // w for hi, base in zip(his, bases)])
    lane = lax.broadcasted_iota(jnp.int32, (1, 2 * w), 1)
    left = lane < w

    def window_sum(p):
        hots = []
        for e in range(0, n_exp, 2):
            wants, starts = [], []
            for k in range(2):
                want = bases[e + k] + p * w
                st = pl.multiple_of(jnp.minimum(want, cap - w), BF16_ROWS)
                win_ref[(e + k) * w:(e + k + 1) * w, :] = y_ref[e + k, 0, pl.ds(st, w), :]
                wants.append(want)
                starts.append(st)
            slot = jnp.where(left, starts[0] + lane, starts[1] + lane - w)
            slot = jnp.where(slot >= jnp.where(left, wants[0], wants[1]), slot, -2).astype(F32)
            tok_pos = jnp.where(left, pos_ref[0, :, e:e + 1], pos_ref[0, :, e + 1:e + 2])
            hots.append(jnp.where(tok_pos == slot, 1.0, 0.0).astype(BF16))
        return _dot(jnp.concatenate(hots, axis=1), win_ref[...])

    acc = lax.fori_loop(1, n_pass, lambda p, a: a + window_sum(p), window_sum(0))
    h = h_ref[0] + gate_ref[0] * acc
    if final:
        h = h * lax.rsqrt(jnp.mean(h * h, axis=-1, keepdims=True) + EPS) * fg_ref[...]
    o_ref[0] = h


def moe_combine(tile_start, pos_t, y, h, gate, final_g, *, tt, final):
    b, n, d = h.shape
    e, _, cap, _ = y.shape
    grid_spec = pltpu.PrefetchScalarGridSpec(
        num_scalar_prefetch=1,
        grid=(b, n // tt),
        in_specs=[pl.BlockSpec((1, tt, e), lambda bi, i, s: (bi, i, 0)),
                  pl.BlockSpec((e, 1, cap, d), lambda bi, i, s: (0, bi, 0, 0), pipeline_mode=pl.Buffered(1)),
                  pl.BlockSpec((1, tt, d), lambda bi, i, s: (bi, i, 0)),
                  pl.BlockSpec((1, 1, d), lambda bi, i, s: (bi, 0, 0)),
                  pl.BlockSpec((1, d), lambda bi, i, s: (0, 0))],
        out_specs=pl.BlockSpec((1, tt, d), lambda bi, i, s: (bi, i, 0)),
        scratch_shapes=[pltpu.VMEM((e * COMBINE_WINDOW, d), BF16)])
    return pl.pallas_call(
        functools.partial(_combine_body, tt=tt, final=final),
        grid_spec=grid_spec,
        out_shape=jax.ShapeDtypeStruct((b, n, d), F32),
        compiler_params=_params(("arbitrary", "arbitrary")),
        name="moe_combine",
    )(tile_start, pos_t, y, h, gate, final_g)


def _expert_choice_moe(h, g, shift, scale, gate2, router, w1, w3, w2, layer, final_g, *, final):
    b, n, d = h.shape
    e = router.shape[1]
    cap = CAPACITY_FACTOR * n // e
    a, aff_t = moe_router(h, g, shift, scale, router, tm=MATMUL_ROWS)
    pos, idx, gate, start = moe_select(aff_t, cap)
    rows = idx[:, :, 0, :] + (jnp.arange(b, dtype=jnp.int32) * n)[:, None, None]
    rows = rows.transpose(1, 0, 2).reshape(e, 1, b * cap)
    gate_e = gate[:, :, 0, :].transpose(1, 0, 2).reshape(e, b * cap, 1)
    y = moe_ffn(rows, a.reshape(b * n, d), gate_e, w1, w3, w2, layer, tf=FFN_TILE)
    stride = COMBINE_TOKENS // LANES
    tile_start = start[:, :, 0, 0:n // LANES + 1:stride].astype(jnp.int32)
    pos_t = pos[:, :, 0, :].transpose(0, 2, 1)
    return moe_combine(tile_start, pos_t, y.reshape(e, b, cap, d), h, gate2, final_g,
                       tt=COMBINE_TOKENS, final=final)


def kernel(x, c, ctx, c_ctx, mod_w, mod_b, norm1_g, norm2_g, router, w1, w3, w2, attn_w_in, attn_w_out,
           na_rpb, q_norm_g, k_norm_g, sgu_w_in, sgu_w_out, sgu_ws, sgu_b, sgu_ln_g, sgu_ln_b, final_norm_g):
    b, n, d = x.shape
    cvec = jnp.concatenate([c, c_ctx[None], jnp.zeros((8 - b - 1, d), c.dtype)], axis=0)
    mods = adaln(cvec, mod_w, mod_b)
    cos, sin = rope_tables(n)
    h = x
    for l in range(DEPTH):
        mod = [mods[l, :, k * d:(k + 1) * d] for k in range(N_MOD)]
        sh1, sc1, g1, sh2, sc2, g2 = (m[:b, None, :] for m in mod)
        n1 = norm1_g[l].reshape(1, d)
        if l % 2 == 0:
            ev = l // 2
            csh1, csc1 = (jnp.broadcast_to(m[b][None, None, :], (b, 1, d)) for m in mod[:2])
            w_in = attn_w_in[ev].astype(BF16)
            proj = norm_mod_matmul(h, n1, sh1, sc1, w_in)
            ctxp = norm_mod_matmul(ctx, n1, csh1, csc1, w_in[:, Q_COLS:])
            tl, tr = na_bias_tables(na_rpb[ev])
            heads_a = neighbourhood_attention(proj, ctxp, tl, tr)
            heads_b = gqa_attention(proj, ctxp, q_norm_g[ev].reshape(1, -1), k_norm_g[ev].reshape(1, -1),
                                    cos, sin, tq=GQA_Q_TILE)
            w_out = attn_w_out[ev].astype(BF16)
            h = matmul_residual([(heads_a, w_out[:NA_W]), (heads_b, w_out[NA_W:])], h, g1)
        else:
            o = l // 2
            z = norm_mod_matmul(h, n1, sh1, sc1, sgu_w_in[o].astype(BF16), gelu=True)
            mixed = sgu_mix(z, sgu_ws[o], sgu_b[o], sgu_ln_g[o], sgu_ln_b[o])
            h = matmul_residual([(mixed, sgu_w_out[o].astype(BF16))], h, g1)
        h = _expert_choice_moe(h, norm2_g[l].reshape(1, d), sh2, sc2, g2, router[l], w1, w3, w2, l,
                               final_norm_g.reshape(1, d), final=(l == DEPTH - 1))
    return h
```

```python
import functools

import jax
import jax.numpy as jnp
from jax import lax
from jax.experimental import pallas as pl
from jax.experimental.pallas import tpu as pltpu

D_MODEL = 2048
DEPTH = 2
GRID_W = 64
CTX_LEN = 256
HEAD_DIM = 128
ATTN_SCALE = HEAD_DIM ** -0.5
NA_HEADS = 8
NA_KH = 8
NA_KW = 16
GQA_Q_HEADS = 8
GQA_KV_HEADS = 2
GQA_GROUP = GQA_Q_HEADS // GQA_KV_HEADS
ROPE_THETA = 10000.0
CHUNK = 128
SGU_GROUPS = 8
SGU_WIDTH = 2 * D_MODEL
N_EXPERTS = 16
CAPACITY_FACTOR = 2
N_MOD = 6
EPS = 1e-6
NA_W = NA_HEADS * HEAD_DIM
Q_COLS = NA_W + GQA_Q_HEADS * HEAD_DIM

QA_BLK, QB_BLK, KA_BLK, VA_BLK, KB_BLK, VB_BLK = 0, 8, 16, 24, 32, 34
CKA_BLK, CVA_BLK, CKB_BLK, CVB_BLK = 0, 8, 16, 18

NEG_INF = -1e30
V7X_VMEM_LIMIT_BYTES = 56 * 1024 * 1024
BF16 = jnp.bfloat16
F32 = jnp.float32

NA_Q_ROWS = 4
NA_K_ROWS = 12
NA_MASKED = 2 * NA_KH - 1
LOG2_E = 1.4426950408889634
NA_HEADS_PER_STEP = 4

GQA_KEY_CHUNK = 512
GQA_Q_CHUNK = 256
GQA_Q_TILE = 2048
LANES = 128
BF16_ROWS = 16
FFN_TILE = 256
GATHER_CAST_ROWS = 256
COMBINE_TOKENS = 256
COMBINE_WINDOW = 64
SGU_CHUNKS_PER_STEP = 4
MATMUL_ROWS = 1024
MATMUL_MAX_COLS = 1536
RESIDUAL_COLS = (512, 1024)
RESIDUAL_VMEM_BUDGET = 40 * 1024 * 1024
MXU_COLS = 256


def _col_tile(n_out):
    return max(t for t in range(MXU_COLS, MATMUL_MAX_COLS + 1, MXU_COLS) if n_out % t == 0)


def _params(semantics):
    return pltpu.CompilerParams(dimension_semantics=semantics,
                                vmem_limit_bytes=V7X_VMEM_LIMIT_BYTES)


def _dot(a, b):
    return jnp.dot(a, b, preferred_element_type=F32)


def _dot_nt(a, b):
    return lax.dot_general(a, b, (((1,), (1,)), ((), ())), preferred_element_type=F32)


def _adaln_body(c_ref, w_ref, b_ref, o_ref):
    c = c_ref[...]
    s = (c * jax.nn.sigmoid(c)).astype(BF16)
    o_ref[0] = _dot(s, w_ref[0].astype(BF16)) + b_ref[0]


def adaln(cvec, mod_w, mod_b, tn=1024):
    L, d, n = mod_w.shape
    return pl.pallas_call(
        _adaln_body,
        grid=(L, n // tn),
        in_specs=[pl.BlockSpec((8, d), lambda l, j: (0, 0)),
                  pl.BlockSpec((1, d, tn), lambda l, j: (l, 0, j)),
                  pl.BlockSpec((1, 1, tn), lambda l, j: (l, 0, j))],
        out_specs=pl.BlockSpec((1, 8, tn), lambda l, j: (l, 0, j)),
        out_shape=jax.ShapeDtypeStruct((L, 8, n), F32),
        compiler_params=_params(("parallel", "arbitrary")),
        name="adaln",
    )(cvec, mod_w, mod_b.reshape(L, 1, n))


def _modulated(x, g, sh, sc):
    y = x * lax.rsqrt(jnp.mean(x * x, axis=-1, keepdims=True) + EPS) * g
    return y * (1 + sc) + sh


def _nmm_body(x_ref, g_ref, sh_ref, sc_ref, w_ref, o_ref, a_ref, *, gelu):
    @pl.when(pl.program_id(2) == 0)
    def _():
        a_ref[...] = _modulated(x_ref[0], g_ref[...], sh_ref[0], sc_ref[0]).astype(BF16)

    z = _dot(a_ref[...], w_ref[...])
    if gelu:
        z = jax.nn.gelu(z)
    o_ref[0] = z.astype(o_ref.dtype)


def norm_mod_matmul(h, g, shift, scale, w, *, gelu=False):
    b, n, d = h.shape
    n_out = w.shape[1]
    tm = min(n, MATMUL_ROWS)
    tn = _col_tile(n_out)
    return pl.pallas_call(
        functools.partial(_nmm_body, gelu=gelu),
        grid=(b, n // tm, n_out // tn),
        in_specs=[pl.BlockSpec((1, tm, d), lambda bi, i, j: (bi, i, 0)),
                  pl.BlockSpec((1, d), lambda bi, i, j: (0, 0)),
                  pl.BlockSpec((1, 1, d), lambda bi, i, j: (bi, 0, 0)),
                  pl.BlockSpec((1, 1, d), lambda bi, i, j: (bi, 0, 0)),
                  pl.BlockSpec((d, tn), lambda bi, i, j: (0, j))],
        out_specs=pl.BlockSpec((1, tm, tn), lambda bi, i, j: (bi, i, j)),
        out_shape=jax.ShapeDtypeStruct((b, n, n_out), BF16),
        scratch_shapes=[pltpu.VMEM((tm, d), BF16)],
        compiler_params=_params(("parallel", "parallel", "arbitrary")),
        name="norm_mod_matmul",
    )(h, g, shift, scale, w)


def _mmr_body(*refs, n_parts):
    a_refs, w_refs = refs[:n_parts], refs[n_parts:2 * n_parts]
    h_ref, gate_ref, o_ref = refs[2 * n_parts:]
    acc = _dot(a_refs[0][0], w_refs[0][...])
    for a_ref, w_ref in zip(a_refs[1:], w_refs[1:]):
        acc = acc + _dot(a_ref[0], w_ref[...])
    o_ref[0] = h_ref[0] + gate_ref[0] * acc


def matmul_residual(parts, h, gate):
    b, n, d = h.shape
    tm = min(n, MATMUL_ROWS)
    k_total = sum(w.shape[0] for _, w in parts)

    def step_bytes(tn):
        return 2 * (2 * tm * k_total + 2 * k_total * tn + 4 * tm * tn + 4 * tm * tn)

    tn = max(t for t in RESIDUAL_COLS if d % t == 0 and step_bytes(t) <= RESIDUAL_VMEM_BUDGET)
    a_list = [a for a, _ in parts]
    w_list = [w for _, w in parts]
    return pl.pallas_call(
        functools.partial(_mmr_body, n_parts=len(parts)),
        grid=(b, n // tm, d // tn),
        in_specs=([pl.BlockSpec((1, tm, a.shape[2]), lambda bi, i, j: (bi, i, 0)) for a in a_list]
                  + [pl.BlockSpec((w.shape[0], tn), lambda bi, i, j: (0, j)) for w in w_list]
                  + [pl.BlockSpec((1, tm, tn), lambda bi, i, j: (bi, i, j)),
                     pl.BlockSpec((1, 1, tn), lambda bi, i, j: (bi, 0, j))]),
        out_specs=pl.BlockSpec((1, tm, tn), lambda bi, i, j: (bi, i, j)),
        out_shape=jax.ShapeDtypeStruct((b, n, d), F32),
        compiler_params=_params(("parallel", "parallel", "arbitrary")),
        name="matmul_residual",
    )(*a_list, *w_list, h, gate)


def na_bias_tables(rpb):
    cols = jnp.arange(GRID_W)
    start = jnp.clip(cols - NA_KW // 2, 0, GRID_W - NA_KW)
    j = jnp.arange(GRID_W)
    inside = (j[None, :] >= start[:, None]) & (j[None, :] < start[:, None] + NA_KW)
    dcol = jnp.clip(j[None, :] - cols[:, None] + NA_KW - 1, 0, 2 * NA_KW - 2)
    t = jnp.where(inside[None, None], rpb[:, :, dcol] * LOG2_E, NEG_INF)
    t = jnp.concatenate([t, jnp.full_like(t[:, :1], NEG_INF)], axis=1)
    z = jnp.zeros_like(t)
    return jnp.concatenate([t, z], axis=-1), jnp.concatenate([z, t], axis=-1)


def _na_body(q_ref, k_ref, v_ref, kc_ref, vc_ref, tl_ref, tr_ref, o_ref, s_ref):
    rows = k_ref.shape[1] // GRID_W
    r0 = pl.program_id(2) * NA_Q_ROWS
    k0 = jnp.clip(r0 - NA_KH // 2, 0, rows - NA_K_ROWS)
    kstart = pl.multiple_of(k0 * GRID_W, GRID_W)
    keys = pl.ds(kstart, NA_K_ROWS * GRID_W)
    scale = ATTN_SCALE * LOG2_E

    def table_index(kr, r, rs):
        valid = (kr >= rs) & (kr < rs + NA_KH)
        return jnp.where(valid, kr - r + NA_KH - 1, NA_MASKED)

    table_ids = []
    for qr in range(NA_Q_ROWS):
        r = r0 + qr
        rs = jnp.clip(r - NA_KH // 2, 0, rows - NA_KH)
        table_ids.append([(table_index(k0 + 2 * p, r, rs), table_index(k0 + 2 * p + 1, r, rs))
                          for p in range(NA_K_ROWS // 2)])

    for hh in range(NA_HEADS_PER_STEP):
        lanes = slice(hh * HEAD_DIM, (hh + 1) * HEAD_DIM)
        q = q_ref[0, :, lanes]
        s_ref[hh] = _dot_nt(q, k_ref[0, keys, lanes]) * scale
        s_ctx = _dot_nt(q, kc_ref[0, :, lanes]) * scale
        for qr in range(NA_Q_ROWS):
            for p in range(NA_K_ROWS // 2):
                ia, ib = table_ids[qr][p]
                blk = (hh, slice(qr * GRID_W, (qr + 1) * GRID_W), slice(p * 2 * GRID_W, (p + 1) * 2 * GRID_W))
                s_ref[blk] = s_ref[blk] + tl_ref[hh, ia] + tr_ref[hh, ib]
        s_loc = s_ref[hh]
        m = jnp.maximum(jnp.max(s_loc, axis=-1, keepdims=True), jnp.max(s_ctx, axis=-1, keepdims=True))
        p_loc = jnp.exp2(s_loc - m)
        p_ctx = jnp.exp2(s_ctx - m)
        denom = jnp.sum(p_loc, axis=-1, keepdims=True) + jnp.sum(p_ctx, axis=-1, keepdims=True)
        o = _dot(p_loc.astype(BF16), v_ref[0, keys, lanes]) + _dot(p_ctx.astype(BF16), vc_ref[0, :, lanes])
        o_ref[0, :, lanes] = (o / denom).astype(o_ref.dtype)


def neighbourhood_attention(proj, ctxp, tl, tr):
    b, n, _ = proj.shape
    lc = ctxp.shape[1]
    tq = NA_Q_ROWS * GRID_W
    hs = NA_HEADS_PER_STEP
    w = hs * HEAD_DIM
    table = pl.BlockSpec((hs, 2 * NA_KH, GRID_W, 2 * GRID_W), lambda bi, h, t: (h, 0, 0, 0))
    return pl.pallas_call(
        _na_body,
        grid=(b, NA_HEADS // hs, n // tq),
        in_specs=[pl.BlockSpec((1, tq, w), lambda bi, h, t: (bi, t, QA_BLK // hs + h)),
                  pl.BlockSpec((1, n, w), lambda bi, h, t: (bi, 0, KA_BLK // hs + h)),
                  pl.BlockSpec((1, n, w), lambda bi, h, t: (bi, 0, VA_BLK // hs + h)),
                  pl.BlockSpec((1, lc, w), lambda bi, h, t: (bi, 0, CKA_BLK // hs + h)),
                  pl.BlockSpec((1, lc, w), lambda bi, h, t: (bi, 0, CVA_BLK // hs + h)),
                  table, table],
        out_specs=pl.BlockSpec((1, tq, w), lambda bi, h, t: (bi, t, h)),
        out_shape=jax.ShapeDtypeStruct((b, n, NA_W), BF16),
        scratch_shapes=[pltpu.VMEM((hs, tq, NA_K_ROWS * GRID_W), F32)],
        compiler_params=_params(("parallel", "parallel", "arbitrary")),
        name="neighbourhood_attention",
    )(proj, proj, proj, ctxp, ctxp, tl, tr)


def rope_tables(n):
    t = jnp.arange(n)
    row = (t // GRID_W).astype(F32)
    col = (t % GRID_W).astype(F32)
    axis_dims = HEAD_DIM // 2
    inv_freq = ROPE_THETA ** (-jnp.arange(0, axis_dims, 2, dtype=F32) / axis_dims)
    ang_r = row[:, None] * inv_freq
    ang_c = col[:, None] * inv_freq
    cos = jnp.concatenate([jnp.cos(ang_r), jnp.cos(ang_r), jnp.cos(ang_c), jnp.cos(ang_c)], axis=-1)
    sin = jnp.concatenate([-jnp.sin(ang_r), jnp.sin(ang_r), -jnp.sin(ang_c), jnp.sin(ang_c)], axis=-1)
    return cos, sin


def _head_rms(x, g):
    return x * lax.rsqrt(jnp.mean(x * x, axis=-1, keepdims=True) + EPS) * g


def _rope(x, cos, sin):
    quarter = HEAD_DIM // 4
    lane = lax.broadcasted_iota(jnp.int32, x.shape, 1)
    partner = jnp.where(lane % (2 * quarter) < quarter,
                        pltpu.roll(x, HEAD_DIM - quarter, 1), pltpu.roll(x, quarter, 1))
    return x * cos + partner * sin


def _gqa_body(q_ref, k_ref, v_ref, kc_ref, vc_ref, qg_ref, kg_ref, cq_ref, sq_ref, ck_ref, sk_ref,
              o_ref, kt_ref):
    n = k_ref.shape[1]
    lc = kc_ref.shape[1]

    @pl.when(pl.program_id(2) == 0)
    def _():
        kg = kg_ref[...]
        for c in range(n // GQA_KEY_CHUNK):
            rows = slice(c * GQA_KEY_CHUNK, (c + 1) * GQA_KEY_CHUNK)
            kn = _rope(_head_rms(k_ref[0, rows, :].astype(F32), kg), ck_ref[rows, :], sk_ref[rows, :])
            kt_ref[:, rows] = kn.T.astype(BF16)
        kt_ref[:, n:n + lc] = _head_rms(kc_ref[0].astype(F32), kg).T.astype(BF16)

    for c in range(q_ref.shape[1] // GQA_Q_CHUNK):
        rows = slice(c * GQA_Q_CHUNK, (c + 1) * GQA_Q_CHUNK)
        q = _rope(_head_rms(q_ref[0, rows, :].astype(F32), qg_ref[...]), cq_ref[rows, :], sq_ref[rows, :])
        q = (q * (ATTN_SCALE * LOG2_E)).astype(BF16)
        s = _dot(q, kt_ref[...])
        p = jnp.exp2(s - jnp.max(s, axis=-1, keepdims=True))
        denom = jnp.sum(p, axis=-1, keepdims=True)
        pb = p.astype(BF16)
        o = _dot(pb[:, :n], v_ref[0]) + _dot(pb[:, n:], vc_ref[0])
        o_ref[0, rows, :] = (o / denom).astype(o_ref.dtype)


def gqa_attention(proj, ctxp, qn_g, kn_g, cos, sin, *, tq):
    b, n, _ = proj.shape
    lc = ctxp.shape[1]
    hd = HEAD_DIM
    nq = n // tq

    def qhead(kv, i):
        return kv * GQA_GROUP + i // nq

    return pl.pallas_call(
        _gqa_body,
        grid=(b, GQA_KV_HEADS, GQA_GROUP * nq),
        in_specs=[pl.BlockSpec((1, tq, hd), lambda bi, kv, i: (bi, i % nq, QB_BLK + qhead(kv, i))),
                  pl.BlockSpec((1, n, hd), lambda bi, kv, i: (bi, 0, KB_BLK + kv)),
                  pl.BlockSpec((1, n, hd), lambda bi, kv, i: (bi, 0, VB_BLK + kv)),
                  pl.BlockSpec((1, lc, hd), lambda bi, kv, i: (bi, 0, CKB_BLK + kv)),
                  pl.BlockSpec((1, lc, hd), lambda bi, kv, i: (bi, 0, CVB_BLK + kv)),
                  pl.BlockSpec((1, hd), lambda bi, kv, i: (0, 0)),
                  pl.BlockSpec((1, hd), lambda bi, kv, i: (0, 0)),
                  pl.BlockSpec((tq, hd), lambda bi, kv, i: (i % nq, 0)),
                  pl.BlockSpec((tq, hd), lambda bi, kv, i: (i % nq, 0)),
                  pl.BlockSpec((n, hd), lambda bi, kv, i: (0, 0)),
                  pl.BlockSpec((n, hd), lambda bi, kv, i: (0, 0))],
        out_specs=pl.BlockSpec((1, tq, hd), lambda bi, kv, i: (bi, i % nq, qhead(kv, i))),
        out_shape=jax.ShapeDtypeStruct((b, n, GQA_Q_HEADS * hd), BF16),
        scratch_shapes=[pltpu.VMEM((hd, n + lc), BF16)],
        compiler_params=_params(("parallel", "parallel", "arbitrary")),
        name="gqa_attention",
    )(proj, proj, proj, ctxp, ctxp, qn_g, kn_g, cos, sin, cos, sin)


def _sgu_body(z_ref, ws_ref, bs_ref, g_ref, b_ref, o_ref):
    dg = SGU_WIDTH // SGU_GROUPS
    for c in range(z_ref.shape[1] // CHUNK):
        rows = slice(c * CHUNK, (c + 1) * CHUNK)
        v = z_ref[0, rows, SGU_WIDTH:].astype(F32)
        mu = jnp.mean(v, axis=-1, keepdims=True)
        vc = v - mu
        var = jnp.mean(vc * vc, axis=-1, keepdims=True)
        vn = (vc * lax.rsqrt(var + EPS) * g_ref[...] + b_ref[...]).astype(BF16)
        for g in range(SGU_GROUPS):
            cols = slice(g * dg, (g + 1) * dg)
            mixed = _dot(ws_ref[g].astype(BF16), vn[:, cols]) + bs_ref[:, g:g + 1]
            o_ref[0, rows, cols] = (z_ref[0, rows, cols].astype(F32) * mixed).astype(o_ref.dtype)


def sgu_mix(z, ws, bs, ln_g, ln_b):
    b, n, _ = z.shape
    rows = SGU_CHUNKS_PER_STEP * CHUNK
    return pl.pallas_call(
        _sgu_body,
        grid=(b, n // rows),
        in_specs=[pl.BlockSpec((1, rows, 2 * SGU_WIDTH), lambda bi, c: (bi, c, 0)),
                  pl.BlockSpec((SGU_GROUPS, CHUNK, CHUNK), lambda bi, c: (0, 0, 0)),
                  pl.BlockSpec((CHUNK, SGU_GROUPS), lambda bi, c: (0, 0)),
                  pl.BlockSpec((1, SGU_WIDTH), lambda bi, c: (0, 0)),
                  pl.BlockSpec((1, SGU_WIDTH), lambda bi, c: (0, 0))],
        out_specs=pl.BlockSpec((1, rows, SGU_WIDTH), lambda bi, c: (bi, c, 0)),
        out_shape=jax.ShapeDtypeStruct((b, n, SGU_WIDTH), BF16),
        compiler_params=_params(("parallel", "arbitrary")),
        name="sgu_mix",
    )(z, ws, bs.T, ln_g.reshape(1, -1), ln_b.reshape(1, -1))


def _split_bf16(x):
    hi = x.astype(BF16)
    return hi, (x - hi.astype(F32)).astype(BF16)


def _router_body(x_ref, g_ref, sh_ref, sc_ref, rt_ref, a_ref, aff_ref):
    a = _modulated(x_ref[0], g_ref[...], sh_ref[0], sc_ref[0])
    a_ref[0] = a
    a_hi, a_lo = _split_bf16(a)
    r_hi, r_lo = _split_bf16(rt_ref[...])
    logits = _dot_nt(r_hi, a_hi) + (_dot_nt(r_hi, a_lo) + _dot_nt(r_lo, a_hi))
    e = jnp.exp(logits - jnp.max(logits, axis=0, keepdims=True))
    aff_ref[0] = e / jnp.sum(e, axis=0, keepdims=True)


def moe_router(h, g, shift, scale, router, *, tm):
    b, n, d = h.shape
    e = router.shape[1]
    return pl.pallas_call(
        _router_body,
        grid=(b, n // tm),
        in_specs=[pl.BlockSpec((1, tm, d), lambda bi, i: (bi, i, 0)),
                  pl.BlockSpec((1, d), lambda bi, i: (0, 0)),
                  pl.BlockSpec((1, 1, d), lambda bi, i: (bi, 0, 0)),
                  pl.BlockSpec((1, 1, d), lambda bi, i: (bi, 0, 0)),
                  pl.BlockSpec((e, d), lambda bi, i: (0, 0))],
        out_specs=[pl.BlockSpec((1, tm, d), lambda bi, i: (bi, i, 0)),
                   pl.BlockSpec((1, e, tm), lambda bi, i: (bi, 0, i))],
        out_shape=[jax.ShapeDtypeStruct((b, n, d), F32), jax.ShapeDtypeStruct((b, e, n), F32)],
        compiler_params=_params(("parallel", "arbitrary")),
        name="moe_router",
    )(h, g, shift, scale, router.T)


def _row_copy(a_hbm, xf_ref, sem, src_row, dst_row):
    return pltpu.make_async_copy(a_hbm.at[pl.ds(src_row, 1), :], xf_ref.at[pl.ds(dst_row, 1), :], sem.at[0])


def _ffn_body(rows_ref, next_rows_ref, a_hbm, w1_ref, w3_ref, w2_ref, gate_ref, o_ref, xf_ref, xb_ref, hid_ref,
              sem, *, nt, tf):
    e = pl.program_id(0)
    j = pl.program_id(1)
    n_steps = pl.num_programs(1)
    m = xf_ref.shape[0]
    rows_per_step = m // (nt + xf_ref.shape[1] // tf)

    def wait_all_rows():
        def wait(r, carry):
            _row_copy(a_hbm, xf_ref, sem, 0, r).wait()
            return carry

        lax.fori_loop(0, m, wait, 0, unroll=16)

    def prefetch_next_rows():
        first = j * rows_per_step
        for r in range(rows_per_step):
            _row_copy(a_hbm, xf_ref, sem, next_rows_ref[0, 0, first + r], first + r).start()

    @pl.when((e == 0) & (j == 0))
    def _():
        def start(r, carry):
            _row_copy(a_hbm, xf_ref, sem, rows_ref[0, 0, r], r).start()
            return carry

        lax.fori_loop(0, m, start, 0, unroll=8)

    @pl.when(j == 0)
    def _():
        wait_all_rows()

        def cast(i, carry):
            sl = pl.ds(pl.multiple_of(i * GATHER_CAST_ROWS, GATHER_CAST_ROWS), GATHER_CAST_ROWS)
            xb_ref[sl, :] = xf_ref[sl, :].astype(BF16)
            return carry

        lax.fori_loop(0, m // GATHER_CAST_ROWS, cast, 0)

    @pl.when(j < nt)
    def _():
        prefetch_next_rows()
        x = xb_ref[...]
        h1 = _dot(x, w1_ref[0, 0].astype(BF16))
        h3 = _dot(x, w3_ref[0, 0].astype(BF16))
        hid_ref[j] = (h1 * jax.nn.sigmoid(h1) * h3).astype(BF16)

    @pl.when(j >= nt)
    def _():
        prefetch_next_rows()
        acc = _dot(hid_ref[0], w2_ref[0, 0, 0:tf, :].astype(BF16))
        for k in range(1, nt):
            acc = acc + _dot(hid_ref[k], w2_ref[0, 0, k * tf:(k + 1) * tf, :].astype(BF16))
        o_ref[0] = (acc * gate_ref[0]).astype(o_ref.dtype)

    @pl.when((e == pl.num_programs(0) - 1) & (j == n_steps - 1))
    def _():
        wait_all_rows()


def moe_ffn(rows, a, gate, w1, w3, w2, layer, *, tf):
    e, _, m = rows.shape
    d = a.shape[1]
    f = w1.shape[3]
    nt = f // tf
    nd = d // tf
    assert m % (nt + nd) == 0
    return pl.pallas_call(
        functools.partial(_ffn_body, nt=nt, tf=tf),
        grid=(e, nt + nd),
        in_specs=[pl.BlockSpec((1, 1, m), lambda ei, j: (ei, 0, 0), memory_space=pltpu.SMEM),
                  pl.BlockSpec((1, 1, m), lambda ei, j: ((ei + 1) % e, 0, 0), memory_space=pltpu.SMEM),
                  pl.BlockSpec(memory_space=pl.ANY),
                  pl.BlockSpec((1, 1, d, tf), lambda ei, j: (layer, ei, 0, jnp.minimum(j, nt - 1))),
                  pl.BlockSpec((1, 1, d, tf), lambda ei, j: (layer, ei, 0, jnp.minimum(j, nt - 1))),
                  pl.BlockSpec((1, 1, f, tf), lambda ei, j: (layer, ei, 0, jnp.maximum(j - nt, 0))),
                  pl.BlockSpec((1, m, 1), lambda ei, j: (ei, 0, 0))],
        out_specs=pl.BlockSpec((1, m, tf), lambda ei, j: (ei, 0, jnp.maximum(j - nt, 0))),
        out_shape=jax.ShapeDtypeStruct((e, m, d), BF16),
        scratch_shapes=[pltpu.VMEM((m, d), F32), pltpu.VMEM((m, d), BF16), pltpu.VMEM((nt, m, tf), BF16),
                        pltpu.SemaphoreType.DMA((1,))],
        compiler_params=_params(("arbitrary", "arbitrary")),
        name="moe_ffn",
    )(rows, rows, a, w1, w3, w2, gate)


def prefix_constants(n):
    i = jnp.arange(LANES)
    within = (i[:, None] < i[None, :]).astype(BF16)
    tok_tile = jnp.arange(n) // LANES
    before = (tok_tile[:, None] < i[None, :]).astype(BF16)
    return within, before


def _exclusive_prefix(mask, within_ref, before_ref):
    n = mask.shape[1]
    m = mask.astype(F32).astype(BF16)
    tile_start = _dot(m, before_ref[...])
    parts = []
    for t in range(n // LANES):
        cols = slice(t * LANES, (t + 1) * LANES)
        parts.append(_dot(m[:, cols], within_ref[...]) + tile_start[:, t:t + 1])
    return jnp.concatenate(parts, axis=1), tile_start


def _select_body(aff_ref, within_ref, before_ref, pos_ref, idx_ref, gate_ref, start_ref, pos_all, start_all,
                 *, cap):
    e = pl.program_id(1)
    n = aff_ref.shape[2]

    @pl.when(e == 0)
    def _():
        aff = aff_ref[0]

        def refine(i, thr):
            cand = thr | lax.shift_left(jnp.int32(1), 30 - i)
            cnt = jnp.sum((aff >= pltpu.bitcast(cand, F32)).astype(F32), axis=-1, keepdims=True)
            return jnp.where(cnt >= cap, cand, thr)

        thr = lax.fori_loop(0, 31, refine, jnp.zeros((aff.shape[0], 1), jnp.int32))
        above = aff >= pltpu.bitcast(thr + 1, F32)
        tied = (aff >= pltpu.bitcast(thr, F32)) & jnp.logical_not(above)
        need = cap - jnp.sum(above.astype(F32), axis=-1, keepdims=True)
        tied_rank, _ = _exclusive_prefix(tied, within_ref, before_ref)
        sel = above | (tied & (tied_rank < need))
        pos, tile_start = _exclusive_prefix(sel, within_ref, before_ref)
        pos_all[...] = jnp.where(sel, pos, -1.0)
        start_all[...] = tile_start

    a = aff_ref[0, pl.ds(e, 1), :]
    pos = pos_all[pl.ds(e, 1), :]
    pos_ref[0, 0] = pos
    start_ref[0, 0] = start_all[pl.ds(e, 1), :]

    slot = lax.broadcasted_iota(jnp.int32, (cap, n), 0).astype(F32)
    onehot = jnp.where(jnp.broadcast_to(pos, (cap, n)) == slot, 1.0, 0.0).astype(BF16)
    tok = lax.broadcasted_iota(jnp.int32, (8, n), 1)
    row = lax.broadcasted_iota(jnp.int32, (8, n), 0)
    a8 = jnp.broadcast_to(a, (8, n))
    a_hi = a8.astype(BF16).astype(F32)
    a_mid = (a8 - a_hi).astype(BF16).astype(F32)
    a_lo = a8 - a_hi - a_mid
    table = jnp.where(row == 0, lax.shift_right_logical(tok, 6).astype(F32),
                      jnp.where(row == 1, (tok & 63).astype(F32),
                                jnp.where(row == 2, a_hi,
                                          jnp.where(row == 3, a_mid, jnp.where(row == 4, a_lo, 0.0)))))
    picked = _dot_nt(table.astype(BF16), onehot)
    idx_ref[0, 0] = (picked[0:1] * 64.0 + picked[1:2]).astype(jnp.int32)
    gate_ref[0, 0] = picked[2:3] + picked[3:4] + picked[4:5]


def moe_select(aff_t, cap):
    b, e, n = aff_t.shape
    within, before = prefix_constants(n)
    row = lambda last: pl.BlockSpec((1, 1, 1, last), lambda bi, ei: (bi, ei, 0, 0))
    return pl.pallas_call(
        functools.partial(_select_body, cap=cap),
        grid=(b, e),
        in_specs=[pl.BlockSpec((1, e, n), lambda bi, ei: (bi, 0, 0)),
                  pl.BlockSpec((LANES, LANES), lambda bi, ei: (0, 0)),
                  pl.BlockSpec((n, LANES), lambda bi, ei: (0, 0))],
        out_specs=[row(n), row(cap), row(cap), row(LANES)],
        out_shape=[jax.ShapeDtypeStruct((b, e, 1, n), F32), jax.ShapeDtypeStruct((b, e, 1, cap), jnp.int32),
                   jax.ShapeDtypeStruct((b, e, 1, cap), F32), jax.ShapeDtypeStruct((b, e, 1, LANES), F32)],
        scratch_shapes=[pltpu.VMEM((e, n), F32), pltpu.VMEM((e, LANES), F32)],
        compiler_params=_params(("parallel", "arbitrary")),
        name="moe_select",
    )(aff_t, within, before)


def _combine_body(start_ref, pos_ref, y_ref, h_ref, gate_ref, fg_ref, o_ref, win_ref, *, tt, final):
    bi = pl.program_id(0)
    ti = pl.program_id(1)
    n_exp, _, cap, d = y_ref.shape
    w = COMBINE_WINDOW

    los = [start_ref[bi, e, ti] for e in range(n_exp)]
    his = [start_ref[bi, e, ti + 1] for e in range(n_exp)]
    bases = [(lo // BF16_ROWS) * BF16_ROWS for lo in los]
    n_pass = functools.reduce(jnp.maximum, [(hi - base + w - 1) // w for hi, base in zip(his, bases)])
    lane = lax.broadcasted_iota(jnp.int32, (1, 2 * w), 1)
    left = lane < w
    pos_t = pos_ref[0].T

    def window_sum(p):
        hots = []
        for e in range(0, n_exp, 2):
            wants, starts = [], []
            for k in range(2):
                want = bases[e + k] + p * w
                st = pl.multiple_of(jnp.minimum(want, cap - w), BF16_ROWS)
                win_ref[(e + k) * w:(e + k + 1) * w, :] = y_ref[e + k, 0, pl.ds(st, w), :]
                wants.append(want)
                starts.append(st)
            slot = jnp.where(left, starts[0] + lane, starts[1] + lane - w)
            slot = jnp.where(slot >= jnp.where(left, wants[0], wants[1]), slot, -2).astype(F32)
            tok_pos = jnp.where(left, pos_t[:, e:e + 1], pos_t[:, e + 1:e + 2])
            hots.append(jnp.where(tok_pos == slot, 1.0, 0.0).astype(BF16))
        hot = jnp.concatenate(hots, axis=1)
        half = tt // 2
        return jnp.concatenate([_dot(hot[:half], win_ref[...]), _dot(hot[half:], win_ref[...])], axis=0)

    acc = lax.fori_loop(1, n_pass, lambda p, a: a + window_sum(p), window_sum(0))
    h = h_ref[0] + gate_ref[0] * acc
    if final:
        h = h * lax.rsqrt(jnp.mean(h * h, axis=-1, keepdims=True) + EPS) * fg_ref[...]
    o_ref[0] = h


def moe_combine(tile_start, pos, y, h, gate, final_g, *, tt, final):
    b, n, d = h.shape
    e, _, cap, _ = y.shape
    grid_spec = pltpu.PrefetchScalarGridSpec(
        num_scalar_prefetch=1,
        grid=(b, n // tt),
        in_specs=[pl.BlockSpec((1, e, tt), lambda bi, i, s: (bi, 0, i)),
                  pl.BlockSpec((e, 1, cap, d), lambda bi, i, s: (0, bi, 0, 0), pipeline_mode=pl.Buffered(1)),
                  pl.BlockSpec((1, tt, d), lambda bi, i, s: (bi, i, 0)),
                  pl.BlockSpec((1, 1, d), lambda bi, i, s: (bi, 0, 0)),
                  pl.BlockSpec((1, d), lambda bi, i, s: (0, 0))],
        out_specs=pl.BlockSpec((1, tt, d), lambda bi, i, s: (bi, i, 0)),
        scratch_shapes=[pltpu.VMEM((e * COMBINE_WINDOW, d), BF16)])
    return pl.pallas_call(
        functools.partial(_combine_body, tt=tt, final=final),
        grid_spec=grid_spec,
        out_shape=jax.ShapeDtypeStruct((b, n, d), F32),
        compiler_params=_params(("arbitrary", "arbitrary")),
        name="moe_combine",
    )(tile_start, pos, y, h, gate, final_g)


def _expert_choice_moe(h, g, shift, scale, gate2, router, w1, w3, w2, layer, final_g, *, final):
    b, n, d = h.shape
    e = router.shape[1]
    cap = CAPACITY_FACTOR * n // e
    a, aff_t = moe_router(h, g, shift, scale, router, tm=MATMUL_ROWS)
    pos, idx, gate, start = moe_select(aff_t, cap)
    rows = idx[:, :, 0, :] + (jnp.arange(b, dtype=jnp.int32) * n)[:, None, None]
    rows = rows.transpose(1, 0, 2).reshape(e, 1, b * cap)
    gate_e = gate[:, :, 0, :].transpose(1, 0, 2).reshape(e, b * cap, 1)
    y = moe_ffn(rows, a.reshape(b * n, d), gate_e, w1, w3, w2, layer, tf=FFN_TILE)
    stride = COMBINE_TOKENS // LANES
    tile_start = start[:, :, 0, 0:n // LANES + 1:stride].astype(jnp.int32)
    return moe_combine(tile_start, pos[:, :, 0, :], y.reshape(e, b, cap, d), h, gate2, final_g,
                       tt=COMBINE_TOKENS, final=final)


def kernel(x, c, ctx, c_ctx, mod_w, mod_b, norm1_g, norm2_g, router, w1, w3, w2, attn_w_in, attn_w_out,
           na_rpb, q_norm_g, k_norm_g, sgu_w_in, sgu_w_out, sgu_ws, sgu_b, sgu_ln_g, sgu_ln_b, final_norm_g):
    b, n, d = x.shape
    cvec = jnp.concatenate([c, c_ctx[None], jnp.zeros((8 - b - 1, d), c.dtype)], axis=0)
    mods = adaln(cvec, mod_w, mod_b)
    cos, sin = rope_tables(n)
    h = x
    for l in range(DEPTH):
        mod = [mods[l, :, k * d:(k + 1) * d] for k in range(N_MOD)]
        sh1, sc1, g1, sh2, sc2, g2 = (m[:b, None, :] for m in mod)
        n1 = norm1_g[l].reshape(1, d)
        if l % 2 == 0:
            ev = l // 2
            csh1, csc1 = (jnp.broadcast_to(m[b][None, None, :], (b, 1, d)) for m in mod[:2])
            w_in = attn_w_in[ev].astype(BF16)
            proj = norm_mod_matmul(h, n1, sh1, sc1, w_in)
            ctxp = norm_mod_matmul(ctx, n1, csh1, csc1, w_in[:, Q_COLS:])
            tl, tr = na_bias_tables(na_rpb[ev])
            heads_a = neighbourhood_attention(proj, ctxp, tl, tr)
            heads_b = gqa_attention(proj, ctxp, q_norm_g[ev].reshape(1, -1), k_norm_g[ev].reshape(1, -1),
                                    cos, sin, tq=GQA_Q_TILE)
            w_out = attn_w_out[ev].astype(BF16)
            h = matmul_residual([(heads_a, w_out[:NA_W]), (heads_b, w_out[NA_W:])], h, g1)
        else:
            o = l // 2
            z = norm_mod_matmul(h, n1, sh1, sc1, sgu_w_in[o].astype(BF16), gelu=True)
            mixed = sgu_mix(z, sgu_ws[o], sgu_b[o], sgu_ln_g[o], sgu_ln_b[o])
            h = matmul_residual([(mixed, sgu_w_out[o].astype(BF16))], h, g1)
        h = _expert_choice_moe(h, norm2_g[l].reshape(1, d), sh2, sc2, g2, router[l], w1, w3, w2, l,
                               final_norm_g.reshape(1, d), final=(l == DEPTH - 1))
    return h
```

```python
import functools

import jax
import jax.numpy as jnp
from jax import lax
from jax.experimental import pallas as pl
from jax.experimental.pallas import tpu as pltpu

D_MODEL = 2048
DEPTH = 2
GRID_W = 64
CTX_LEN = 256
HEAD_DIM = 128
ATTN_SCALE = HEAD_DIM ** -0.5
NA_HEADS = 8
NA_KH = 8
NA_KW = 16
GQA_Q_HEADS = 8
GQA_KV_HEADS = 2
GQA_GROUP = GQA_Q_HEADS // GQA_KV_HEADS
ROPE_THETA = 10000.0
CHUNK = 128
SGU_GROUPS = 8
SGU_WIDTH = 2 * D_MODEL
N_EXPERTS = 16
CAPACITY_FACTOR = 2
N_MOD = 6
EPS = 1e-6
NA_W = NA_HEADS * HEAD_DIM
Q_COLS = NA_W + GQA_Q_HEADS * HEAD_DIM

QA_BLK, QB_BLK, KA_BLK, VA_BLK, KB_BLK, VB_BLK = 0, 8, 16, 24, 32, 34
CKA_BLK, CVA_BLK, CKB_BLK, CVB_BLK = 0, 8, 16, 18

NEG_INF = -1e30
V7X_VMEM_LIMIT_BYTES = 56 * 1024 * 1024
BF16 = jnp.bfloat16
F32 = jnp.float32

NA_Q_ROWS = 4
NA_K_ROWS = 12
NA_MASKED = 2 * NA_KH - 1
LOG2_E = 1.4426950408889634
NA_HEADS_PER_STEP = 4

GQA_KEY_CHUNK = 512
GQA_Q_CHUNK = 256
GQA_Q_TILE = 2048
LANES = 128
SUBLANES = 8
F32_MAGNITUDE_BITS = 31
TOKEN_SPLIT_BITS = 6
BF16_ROWS = 16
FFN_TILE = 256
GATHER_CAST_ROWS = 256
COMBINE_TOKENS = 256
COMBINE_WINDOW = 64
SGU_CHUNKS_PER_STEP = 4
MATMUL_ROWS = 1024
MATMUL_MAX_COLS = 1536
RESIDUAL_COLS = (512, 1024)
RESIDUAL_VMEM_BUDGET = 40 * 1024 * 1024
MXU_COLS = 256


def _col_tile(n_out):
    return max(t for t in range(MXU_COLS, MATMUL_MAX_COLS + 1, MXU_COLS) if n_out % t == 0)


def _params(semantics):
    return pltpu.CompilerParams(dimension_semantics=semantics,
                                vmem_limit_bytes=V7X_VMEM_LIMIT_BYTES)


def _dot(a, b):
    return jnp.dot(a, b, preferred_element_type=F32)


def _dot_nt(a, b):
    return lax.dot_general(a, b, (((1,), (1,)), ((), ())), preferred_element_type=F32)


def _adaln_body(c_ref, w_ref, b_ref, o_ref):
    c = c_ref[...]
    s = (c * jax.nn.sigmoid(c)).astype(BF16)
    o_ref[0] = _dot(s, w_ref[0].astype(BF16)) + b_ref[0]


def adaln(cvec, mod_w, mod_b, tn=1024):
    L, d, n = mod_w.shape
    return pl.pallas_call(
        _adaln_body,
        grid=(L, n // tn),
        in_specs=[pl.BlockSpec((SUBLANES, d), lambda l, j: (0, 0)),
                  pl.BlockSpec((1, d, tn), lambda l, j: (l, 0, j)),
                  pl.BlockSpec((1, 1, tn), lambda l, j: (l, 0, j))],
        out_specs=pl.BlockSpec((1, SUBLANES, tn), lambda l, j: (l, 0, j)),
        out_shape=jax.ShapeDtypeStruct((L, SUBLANES, n), F32),
        compiler_params=_params(("parallel", "arbitrary")),
        name="adaln",
    )(cvec, mod_w, mod_b.reshape(L, 1, n))


def _modulated(x, g, sh, sc):
    y = x * lax.rsqrt(jnp.mean(x * x, axis=-1, keepdims=True) + EPS) * g
    return y * (1 + sc) + sh


def _nmm_body(x_ref, g_ref, sh_ref, sc_ref, w_ref, o_ref, a_ref, *, gelu):
    @pl.when(pl.program_id(2) == 0)
    def _():
        a_ref[...] = _modulated(x_ref[0], g_ref[...], sh_ref[0], sc_ref[0]).astype(BF16)

    z = _dot(a_ref[...], w_ref[...])
    if gelu:
        z = jax.nn.gelu(z)
    o_ref[0] = z.astype(o_ref.dtype)


def norm_mod_matmul(h, g, shift, scale, w, *, gelu=False):
    b, n, d = h.shape
    n_out = w.shape[1]
    tm = min(n, MATMUL_ROWS)
    tn = _col_tile(n_out)
    return pl.pallas_call(
        functools.partial(_nmm_body, gelu=gelu),
        grid=(b, n // tm, n_out // tn),
        in_specs=[pl.BlockSpec((1, tm, d), lambda bi, i, j: (bi, i, 0)),
                  pl.BlockSpec((1, d), lambda bi, i, j: (0, 0)),
                  pl.BlockSpec((1, 1, d), lambda bi, i, j: (bi, 0, 0)),
                  pl.BlockSpec((1, 1, d), lambda bi, i, j: (bi, 0, 0)),
                  pl.BlockSpec((d, tn), lambda bi, i, j: (0, j))],
        out_specs=pl.BlockSpec((1, tm, tn), lambda bi, i, j: (bi, i, j)),
        out_shape=jax.ShapeDtypeStruct((b, n, n_out), BF16),
        scratch_shapes=[pltpu.VMEM((tm, d), BF16)],
        compiler_params=_params(("parallel", "parallel", "arbitrary")),
        name="norm_mod_matmul",
    )(h, g, shift, scale, w)


def _mmr_body(*refs, n_parts):
    a_refs, w_refs = refs[:n_parts], refs[n_parts:2 * n_parts]
    h_ref, gate_ref, o_ref = refs[2 * n_parts:]
    acc = _dot(a_refs[0][0], w_refs[0][...])
    for a_ref, w_ref in zip(a_refs[1:], w_refs[1:]):
        acc = acc + _dot(a_ref[0], w_ref[...])
    o_ref[0] = h_ref[0] + gate_ref[0] * acc


def matmul_residual(parts, h, gate):
    b, n, d = h.shape
    tm = min(n, MATMUL_ROWS)
    k_total = sum(w.shape[0] for _, w in parts)

    def step_bytes(tn):
        return 2 * (2 * tm * k_total + 2 * k_total * tn + 4 * tm * tn + 4 * tm * tn)

    tn = max(t for t in RESIDUAL_COLS if d % t == 0 and step_bytes(t) <= RESIDUAL_VMEM_BUDGET)
    a_list = [a for a, _ in parts]
    w_list = [w for _, w in parts]
    return pl.pallas_call(
        functools.partial(_mmr_body, n_parts=len(parts)),
        grid=(b, n // tm, d // tn),
        in_specs=([pl.BlockSpec((1, tm, a.shape[2]), lambda bi, i, j: (bi, i, 0)) for a in a_list]
                  + [pl.BlockSpec((w.shape[0], tn), lambda bi, i, j: (0, j)) for w in w_list]
                  + [pl.BlockSpec((1, tm, tn), lambda bi, i, j: (bi, i, j)),
                     pl.BlockSpec((1, 1, tn), lambda bi, i, j: (bi, 0, j))]),
        out_specs=pl.BlockSpec((1, tm, tn), lambda bi, i, j: (bi, i, j)),
        out_shape=jax.ShapeDtypeStruct((b, n, d), F32),
        compiler_params=_params(("parallel", "parallel", "arbitrary")),
        name="matmul_residual",
    )(*a_list, *w_list, h, gate)


def na_bias_tables(rpb):
    cols = jnp.arange(GRID_W)
    start = jnp.clip(cols - NA_KW // 2, 0, GRID_W - NA_KW)
    j = jnp.arange(GRID_W)
    inside = (j[None, :] >= start[:, None]) & (j[None, :] < start[:, None] + NA_KW)
    dcol = jnp.clip(j[None, :] - cols[:, None] + NA_KW - 1, 0, 2 * NA_KW - 2)
    t = jnp.where(inside[None, None], rpb[:, :, dcol] * LOG2_E, NEG_INF)
    t = jnp.concatenate([t, jnp.full_like(t[:, :1], NEG_INF)], axis=1)
    z = jnp.zeros_like(t)
    return jnp.concatenate([t, z], axis=-1), jnp.concatenate([z, t], axis=-1)


def _na_body(q_ref, k_ref, v_ref, kc_ref, vc_ref, tl_ref, tr_ref, o_ref, s_ref):
    rows = k_ref.shape[1] // GRID_W
    r0 = pl.program_id(2) * NA_Q_ROWS
    k0 = jnp.clip(r0 - NA_KH // 2, 0, rows - NA_K_ROWS)
    kstart = pl.multiple_of(k0 * GRID_W, GRID_W)
    keys = pl.ds(kstart, NA_K_ROWS * GRID_W)
    scale = ATTN_SCALE * LOG2_E

    def table_index(kr, r, rs):
        valid = (kr >= rs) & (kr < rs + NA_KH)
        return jnp.where(valid, kr - r + NA_KH - 1, NA_MASKED)

    table_ids = []
    for qr in range(NA_Q_ROWS):
        r = r0 + qr
        rs = jnp.clip(r - NA_KH // 2, 0, rows - NA_KH)
        table_ids.append([(table_index(k0 + 2 * p, r, rs), table_index(k0 + 2 * p + 1, r, rs))
                          for p in range(NA_K_ROWS // 2)])

    for hh in range(NA_HEADS_PER_STEP):
        lanes = slice(hh * HEAD_DIM, (hh + 1) * HEAD_DIM)
        q = q_ref[0, :, lanes]
        s_ref[hh] = _dot_nt(q, k_ref[0, keys, lanes]) * scale
        s_ctx = _dot_nt(q, kc_ref[0, :, lanes]) * scale
        for qr in range(NA_Q_ROWS):
            for p in range(NA_K_ROWS // 2):
                ia, ib = table_ids[qr][p]
                blk = (hh, slice(qr * GRID_W, (qr + 1) * GRID_W), slice(p * 2 * GRID_W, (p + 1) * 2 * GRID_W))
                s_ref[blk] = s_ref[blk] + tl_ref[hh, ia] + tr_ref[hh, ib]
        s_loc = s_ref[hh]
        m = jnp.maximum(jnp.max(s_loc, axis=-1, keepdims=True), jnp.max(s_ctx, axis=-1, keepdims=True))
        p_loc = jnp.exp2(s_loc - m)
        p_ctx = jnp.exp2(s_ctx - m)
        denom = jnp.sum(p_loc, axis=-1, keepdims=True) + jnp.sum(p_ctx, axis=-1, keepdims=True)
        o = _dot(p_loc.astype(BF16), v_ref[0, keys, lanes]) + _dot(p_ctx.astype(BF16), vc_ref[0, :, lanes])
        o_ref[0, :, lanes] = (o / denom).astype(o_ref.dtype)


def neighbourhood_attention(proj, ctxp, tl, tr):
    b, n, _ = proj.shape
    lc = ctxp.shape[1]
    tq = NA_Q_ROWS * GRID_W
    hs = NA_HEADS_PER_STEP
    w = hs * HEAD_DIM
    table = pl.BlockSpec((hs, 2 * NA_KH, GRID_W, 2 * GRID_W), lambda bi, h, t: (h, 0, 0, 0))
    return pl.pallas_call(
        _na_body,
        grid=(b, NA_HEADS // hs, n // tq),
        in_specs=[pl.BlockSpec((1, tq, w), lambda bi, h, t: (bi, t, QA_BLK // hs + h)),
                  pl.BlockSpec((1, n, w), lambda bi, h, t: (bi, 0, KA_BLK // hs + h)),
                  pl.BlockSpec((1, n, w), lambda bi, h, t: (bi, 0, VA_BLK // hs + h)),
                  pl.BlockSpec((1, lc, w), lambda bi, h, t: (bi, 0, CKA_BLK // hs + h)),
                  pl.BlockSpec((1, lc, w), lambda bi, h, t: (bi, 0, CVA_BLK // hs + h)),
                  table, table],
        out_specs=pl.BlockSpec((1, tq, w), lambda bi, h, t: (bi, t, h)),
        out_shape=jax.ShapeDtypeStruct((b, n, NA_W), BF16),
        scratch_shapes=[pltpu.VMEM((hs, tq, NA_K_ROWS * GRID_W), F32)],
        compiler_params=_params(("parallel", "parallel", "arbitrary")),
        name="neighbourhood_attention",
    )(proj, proj, proj, ctxp, ctxp, tl, tr)


def rope_tables(n):
    t = jnp.arange(n)
    row = (t // GRID_W).astype(F32)
    col = (t % GRID_W).astype(F32)
    axis_dims = HEAD_DIM // 2
    inv_freq = ROPE_THETA ** (-jnp.arange(0, axis_dims, 2, dtype=F32) / axis_dims)
    ang_r = row[:, None] * inv_freq
    ang_c = col[:, None] * inv_freq
    cos = jnp.concatenate([jnp.cos(ang_r), jnp.cos(ang_r), jnp.cos(ang_c), jnp.cos(ang_c)], axis=-1)
    sin = jnp.concatenate([-jnp.sin(ang_r), jnp.sin(ang_r), -jnp.sin(ang_c), jnp.sin(ang_c)], axis=-1)
    return cos, sin


def _head_rms(x, g):
    return x * lax.rsqrt(jnp.mean(x * x, axis=-1, keepdims=True) + EPS) * g


def _rope(x, cos, sin):
    quarter = HEAD_DIM // 4
    lane = lax.broadcasted_iota(jnp.int32, x.shape, 1)
    partner = jnp.where(lane % (2 * quarter) < quarter,
                        pltpu.roll(x, HEAD_DIM - quarter, 1), pltpu.roll(x, quarter, 1))
    return x * cos + partner * sin


def _gqa_body(q_ref, k_ref, v_ref, kc_ref, vc_ref, qg_ref, kg_ref, cq_ref, sq_ref, ck_ref, sk_ref,
              o_ref, kt_ref):
    n = k_ref.shape[1]
    lc = kc_ref.shape[1]

    @pl.when(pl.program_id(2) == 0)
    def _():
        kg = kg_ref[...]
        for c in range(n // GQA_KEY_CHUNK):
            rows = slice(c * GQA_KEY_CHUNK, (c + 1) * GQA_KEY_CHUNK)
            kn = _rope(_head_rms(k_ref[0, rows, :].astype(F32), kg), ck_ref[rows, :], sk_ref[rows, :])
            kt_ref[:, rows] = kn.T.astype(BF16)
        kt_ref[:, n:n + lc] = _head_rms(kc_ref[0].astype(F32), kg).T.astype(BF16)

    for c in range(q_ref.shape[1] // GQA_Q_CHUNK):
        rows = slice(c * GQA_Q_CHUNK, (c + 1) * GQA_Q_CHUNK)
        q = _rope(_head_rms(q_ref[0, rows, :].astype(F32), qg_ref[...]), cq_ref[rows, :], sq_ref[rows, :])
        q = (q * (ATTN_SCALE * LOG2_E)).astype(BF16)
        s = _dot(q, kt_ref[...])
        p = jnp.exp2(s - jnp.max(s, axis=-1, keepdims=True))
        denom = jnp.sum(p, axis=-1, keepdims=True)
        pb = p.astype(BF16)
        o = _dot(pb[:, :n], v_ref[0]) + _dot(pb[:, n:], vc_ref[0])
        o_ref[0, rows, :] = (o / denom).astype(o_ref.dtype)


def gqa_attention(proj, ctxp, qn_g, kn_g, cos, sin, *, tq):
    b, n, _ = proj.shape
    lc = ctxp.shape[1]
    hd = HEAD_DIM
    nq = n // tq

    def qhead(kv, i):
        return kv * GQA_GROUP + i // nq

    return pl.pallas_call(
        _gqa_body,
        grid=(b, GQA_KV_HEADS, GQA_GROUP * nq),
        in_specs=[pl.BlockSpec((1, tq, hd), lambda bi, kv, i: (bi, i % nq, QB_BLK + qhead(kv, i))),
                  pl.BlockSpec((1, n, hd), lambda bi, kv, i: (bi, 0, KB_BLK + kv)),
                  pl.BlockSpec((1, n, hd), lambda bi, kv, i: (bi, 0, VB_BLK + kv)),
                  pl.BlockSpec((1, lc, hd), lambda bi, kv, i: (bi, 0, CKB_BLK + kv)),
                  pl.BlockSpec((1, lc, hd), lambda bi, kv, i: (bi, 0, CVB_BLK + kv)),
                  pl.BlockSpec((1, hd), lambda bi, kv, i: (0, 0)),
                  pl.BlockSpec((1, hd), lambda bi, kv, i: (0, 0)),
                  pl.BlockSpec((tq, hd), lambda bi, kv, i: (i % nq, 0)),
                  pl.BlockSpec((tq, hd), lambda bi, kv, i: (i % nq, 0)),
                  pl.BlockSpec((n, hd), lambda bi, kv, i: (0, 0)),
                  pl.BlockSpec((n, hd), lambda bi, kv, i: (0, 0))],
        out_specs=pl.BlockSpec((1, tq, hd), lambda bi, kv, i: (bi, i % nq, qhead(kv, i))),
        out_shape=jax.ShapeDtypeStruct((b, n, GQA_Q_HEADS * hd), BF16),
        scratch_shapes=[pltpu.VMEM((hd, n + lc), BF16)],
        compiler_params=_params(("parallel", "parallel", "arbitrary")),
        name="gqa_attention",
    )(proj, proj, proj, ctxp, ctxp, qn_g, kn_g, cos, sin, cos, sin)


def _sgu_body(z_ref, ws_ref, bs_ref, g_ref, b_ref, o_ref):
    dg = SGU_WIDTH // SGU_GROUPS
    for c in range(z_ref.shape[1] // CHUNK):
        rows = slice(c * CHUNK, (c + 1) * CHUNK)
        v = z_ref[0, rows, SGU_WIDTH:].astype(F32)
        mu = jnp.mean(v, axis=-1, keepdims=True)
        vc = v - mu
        var = jnp.mean(vc * vc, axis=-1, keepdims=True)
        vn = (vc * lax.rsqrt(var + EPS) * g_ref[...] + b_ref[...]).astype(BF16)
        for g in range(SGU_GROUPS):
            cols = slice(g * dg, (g + 1) * dg)
            mixed = _dot(ws_ref[g].astype(BF16), vn[:, cols]) + bs_ref[:, g:g + 1]
            o_ref[0, rows, cols] = (z_ref[0, rows, cols].astype(F32) * mixed).astype(o_ref.dtype)


def sgu_mix(z, ws, bs, ln_g, ln_b):
    b, n, _ = z.shape
    rows = SGU_CHUNKS_PER_STEP * CHUNK
    return pl.pallas_call(
        _sgu_body,
        grid=(b, n // rows),
        in_specs=[pl.BlockSpec((1, rows, 2 * SGU_WIDTH), lambda bi, c: (bi, c, 0)),
                  pl.BlockSpec((SGU_GROUPS, CHUNK, CHUNK), lambda bi, c: (0, 0, 0)),
                  pl.BlockSpec((CHUNK, SGU_GROUPS), lambda bi, c: (0, 0)),
                  pl.BlockSpec((1, SGU_WIDTH), lambda bi, c: (0, 0)),
                  pl.BlockSpec((1, SGU_WIDTH), lambda bi, c: (0, 0))],
        out_specs=pl.BlockSpec((1, rows, SGU_WIDTH), lambda bi, c: (bi, c, 0)),
        out_shape=jax.ShapeDtypeStruct((b, n, SGU_WIDTH), BF16),
        compiler_params=_params(("parallel", "arbitrary")),
        name="sgu_mix",
    )(z, ws, bs.T, ln_g.reshape(1, -1), ln_b.reshape(1, -1))


def _split_bf16(x):
    hi = x.astype(BF16)
    return hi, (x - hi.astype(F32)).astype(BF16)


def _router_body(x_ref, g_ref, sh_ref, sc_ref, rt_ref, a_ref, aff_ref):
    a = _modulated(x_ref[0], g_ref[...], sh_ref[0], sc_ref[0])
    a_ref[0] = a
    a_hi, a_lo = _split_bf16(a)
    r_hi, r_lo = _split_bf16(rt_ref[...])
    logits = _dot_nt(r_hi, a_hi) + (_dot_nt(r_hi, a_lo) + _dot_nt(r_lo, a_hi))
    e = jnp.exp(logits - jnp.max(logits, axis=0, keepdims=True))
    aff_ref[0] = e / jnp.sum(e, axis=0, keepdims=True)


def moe_router(h, g, shift, scale, router, *, tm):
    b, n, d = h.shape
    e = router.shape[1]
    return pl.pallas_call(
        _router_body,
        grid=(b, n // tm),
        in_specs=[pl.BlockSpec((1, tm, d), lambda bi, i: (bi, i, 0)),
                  pl.BlockSpec((1, d), lambda bi, i: (0, 0)),
                  pl.BlockSpec((1, 1, d), lambda bi, i: (bi, 0, 0)),
                  pl.BlockSpec((1, 1, d), lambda bi, i: (bi, 0, 0)),
                  pl.BlockSpec((e, d), lambda bi, i: (0, 0))],
        out_specs=[pl.BlockSpec((1, tm, d), lambda bi, i: (bi, i, 0)),
                   pl.BlockSpec((1, e, tm), lambda bi, i: (bi, 0, i))],
        out_shape=[jax.ShapeDtypeStruct((b, n, d), F32), jax.ShapeDtypeStruct((b, e, n), F32)],
        compiler_params=_params(("parallel", "arbitrary")),
        name="moe_router",
    )(h, g, shift, scale, router.T)


def _row_copy(a_hbm, xf_ref, sem, src_row, dst_row):
    return pltpu.make_async_copy(a_hbm.at[pl.ds(src_row, 1), :], xf_ref.at[pl.ds(dst_row, 1), :], sem.at[0])


def _ffn_body(rows_ref, next_rows_ref, a_hbm, w1_ref, w3_ref, w2_ref, gate_ref, o_ref, xf_ref, xb_ref, hid_ref,
              sem, *, nt, tf):
    e = pl.program_id(0)
    j = pl.program_id(1)
    n_steps = pl.num_programs(1)
    m = xf_ref.shape[0]
    rows_per_step = m // (nt + xf_ref.shape[1] // tf)

    def wait_all_rows():
        def wait(r, carry):
            _row_copy(a_hbm, xf_ref, sem, 0, r).wait()
            return carry

        lax.fori_loop(0, m, wait, 0, unroll=16)

    def prefetch_next_rows():
        first = j * rows_per_step
        for r in range(rows_per_step):
            _row_copy(a_hbm, xf_ref, sem, next_rows_ref[0, 0, first + r], first + r).start()

    @pl.when((e == 0) & (j == 0))
    def _():
        def start(r, carry):
            _row_copy(a_hbm, xf_ref, sem, rows_ref[0, 0, r], r).start()
            return carry

        lax.fori_loop(0, m, start, 0, unroll=8)

    @pl.when(j == 0)
    def _():
        wait_all_rows()

        def cast(i, carry):
            sl = pl.ds(pl.multiple_of(i * GATHER_CAST_ROWS, GATHER_CAST_ROWS), GATHER_CAST_ROWS)
            xb_ref[sl, :] = xf_ref[sl, :].astype(BF16)
            return carry

        lax.fori_loop(0, m // GATHER_CAST_ROWS, cast, 0)

    @pl.when(j < nt)
    def _():
        prefetch_next_rows()
        x = xb_ref[...]
        h1 = _dot(x, w1_ref[0, 0].astype(BF16))
        h3 = _dot(x, w3_ref[0, 0].astype(BF16))
        hid_ref[j] = (h1 * jax.nn.sigmoid(h1) * h3).astype(BF16)

    @pl.when(j >= nt)
    def _():
        prefetch_next_rows()
        acc = _dot(hid_ref[0], w2_ref[0, 0, 0:tf, :].astype(BF16))
        for k in range(1, nt):
            acc = acc + _dot(hid_ref[k], w2_ref[0, 0, k * tf:(k + 1) * tf, :].astype(BF16))
        o_ref[0] = (acc * gate_ref[0]).astype(o_ref.dtype)

    @pl.when((e == pl.num_programs(0) - 1) & (j == n_steps - 1))
    def _():
        wait_all_rows()


def moe_ffn(rows, a, gate, w1, w3, w2, layer, *, tf):
    e, _, m = rows.shape
    d = a.shape[1]
    f = w1.shape[3]
    nt = f // tf
    nd = d // tf
    assert m % (nt + nd) == 0
    return pl.pallas_call(
        functools.partial(_ffn_body, nt=nt, tf=tf),
        grid=(e, nt + nd),
        in_specs=[pl.BlockSpec((1, 1, m), lambda ei, j: (ei, 0, 0), memory_space=pltpu.SMEM),
                  pl.BlockSpec((1, 1, m), lambda ei, j: ((ei + 1) % e, 0, 0), memory_space=pltpu.SMEM),
                  pl.BlockSpec(memory_space=pl.ANY),
                  pl.BlockSpec((1, 1, d, tf), lambda ei, j: (layer, ei, 0, jnp.minimum(j, nt - 1))),
                  pl.BlockSpec((1, 1, d, tf), lambda ei, j: (layer, ei, 0, jnp.minimum(j, nt - 1))),
                  pl.BlockSpec((1, 1, f, tf), lambda ei, j: (layer, ei, 0, jnp.maximum(j - nt, 0))),
                  pl.BlockSpec((1, m, 1), lambda ei, j: (ei, 0, 0))],
        out_specs=pl.BlockSpec((1, m, tf), lambda ei, j: (ei, 0, jnp.maximum(j - nt, 0))),
        out_shape=jax.ShapeDtypeStruct((e, m, d), BF16),
        scratch_shapes=[pltpu.VMEM((m, d), F32), pltpu.VMEM((m, d), BF16), pltpu.VMEM((nt, m, tf), BF16),
                        pltpu.SemaphoreType.DMA((1,))],
        compiler_params=_params(("arbitrary", "arbitrary")),
        name="moe_ffn",
    )(rows, rows, a, w1, w3, w2, gate)


def prefix_constants(n):
    i = jnp.arange(LANES)
    within = (i[:, None] < i[None, :]).astype(BF16)
    tok_tile = jnp.arange(n) // LANES
    before = (tok_tile[:, None] < i[None, :]).astype(BF16)
    return within, before


def _exclusive_prefix(mask, within_ref, before_ref):
    n = mask.shape[1]
    m = mask.astype(F32).astype(BF16)
    tile_start = _dot(m, before_ref[...])
    parts = []
    for t in range(n // LANES):
        cols = slice(t * LANES, (t + 1) * LANES)
        parts.append(_dot(m[:, cols], within_ref[...]) + tile_start[:, t:t + 1])
    return jnp.concatenate(parts, axis=1), tile_start


def _select_body(aff_ref, within_ref, before_ref, pos_ref, idx_ref, gate_ref, start_ref, pos_all, start_all,
                 *, cap):
    e = pl.program_id(1)
    n = aff_ref.shape[2]

    @pl.when(e == 0)
    def _():
        aff = aff_ref[0]

        def refine(i, thr):
            cand = thr | lax.shift_left(jnp.int32(1), F32_MAGNITUDE_BITS - 1 - i)
            cnt = jnp.sum((aff >= pltpu.bitcast(cand, F32)).astype(F32), axis=-1, keepdims=True)
            return jnp.where(cnt >= cap, cand, thr)

        thr = lax.fori_loop(0, F32_MAGNITUDE_BITS, refine, jnp.zeros((aff.shape[0], 1), jnp.int32))
        above = aff >= pltpu.bitcast(thr + 1, F32)
        tied = (aff >= pltpu.bitcast(thr, F32)) & jnp.logical_not(above)
        need = cap - jnp.sum(above.astype(F32), axis=-1, keepdims=True)
        tied_rank, _ = _exclusive_prefix(tied, within_ref, before_ref)
        sel = above | (tied & (tied_rank < need))
        pos, tile_start = _exclusive_prefix(sel, within_ref, before_ref)
        pos_all[...] = jnp.where(sel, pos, -1.0)
        start_all[...] = tile_start

    a = aff_ref[0, pl.ds(e, 1), :]
    pos = pos_all[pl.ds(e, 1), :]
    pos_ref[0, 0] = pos
    start_ref[0, 0] = start_all[pl.ds(e, 1), :]

    slot = lax.broadcasted_iota(jnp.int32, (cap, n), 0).astype(F32)
    onehot = jnp.where(jnp.broadcast_to(pos, (cap, n)) == slot, 1.0, 0.0).astype(BF16)
    tok = lax.broadcasted_iota(jnp.int32, (SUBLANES, n), 1)
    row = lax.broadcasted_iota(jnp.int32, (SUBLANES, n), 0)
    a8 = jnp.broadcast_to(a, (SUBLANES, n))
    a_hi = a8.astype(BF16).astype(F32)
    a_mid = (a8 - a_hi).astype(BF16).astype(F32)
    a_lo = a8 - a_hi - a_mid
    table = jnp.where(row == 0, lax.shift_right_logical(tok, TOKEN_SPLIT_BITS).astype(F32),
                      jnp.where(row == 1, (tok & ((1 << TOKEN_SPLIT_BITS) - 1)).astype(F32),
                                jnp.where(row == 2, a_hi,
                                          jnp.where(row == 3, a_mid, jnp.where(row == 4, a_lo, 0.0)))))
    picked = _dot_nt(table.astype(BF16), onehot)
    idx_ref[0, 0] = (picked[0:1] * float(1 << TOKEN_SPLIT_BITS) + picked[1:2]).astype(jnp.int32)
    gate_ref[0, 0] = picked[2:3] + picked[3:4] + picked[4:5]


def moe_select(aff_t, cap):
    b, e, n = aff_t.shape
    within, before = prefix_constants(n)
    row = lambda last: pl.BlockSpec((1, 1, 1, last), lambda bi, ei: (bi, ei, 0, 0))
    return pl.pallas_call(
        functools.partial(_select_body, cap=cap),
        grid=(b, e),
        in_specs=[pl.BlockSpec((1, e, n), lambda bi, ei: (bi, 0, 0)),
                  pl.BlockSpec((LANES, LANES), lambda bi, ei: (0, 0)),
                  pl.BlockSpec((n, LANES), lambda bi, ei: (0, 0))],
        out_specs=[row(n), row(cap), row(cap), row(LANES)],
        out_shape=[jax.ShapeDtypeStruct((b, e, 1, n), F32), jax.ShapeDtypeStruct((b, e, 1, cap), jnp.int32),
                   jax.ShapeDtypeStruct((b, e, 1, cap), F32), jax.ShapeDtypeStruct((b, e, 1, LANES), F32)],
        scratch_shapes=[pltpu.VMEM((e, n), F32), pltpu.VMEM((e, LANES), F32)],
        compiler_params=_params(("parallel", "arbitrary")),
        name="moe_select",
    )(aff_t, within, before)


def _combine_body(start_ref, pos_ref, y_ref, h_ref, gate_ref, fg_ref, o_ref, win_ref, *, tt, final):
    bi = pl.program_id(0)
    ti = pl.program_id(1)
    n_exp, _, cap, d = y_ref.shape
    w = COMBINE_WINDOW

    los = [start_ref[bi, e, ti] for e in range(n_exp)]
    his = [start_ref[bi, e, ti + 1] for e in range(n_exp)]
    bases = [(lo // BF16_ROWS) * BF16_ROWS for lo in los]
    n_pass = functools.reduce(jnp.maximum, [(hi - base + w - 1) // w for hi, base in zip(his, bases)])
    lane = lax.broadcasted_iota(jnp.int32, (1, 2 * w), 1)
    left = lane < w
    pos_t = pos_ref[0].T

    def window_sum(p):
        hots = []
        for e in range(0, n_exp, 2):
            wants, starts = [], []
            for k in range(2):
                want = bases[e + k] + p * w
                st = pl.multiple_of(jnp.minimum(want, cap - w), BF16_ROWS)
                win_ref[(e + k) * w:(e + k + 1) * w, :] = y_ref[e + k, 0, pl.ds(st, w), :]
                wants.append(want)
                starts.append(st)
            slot = jnp.where(left, starts[0] + lane, starts[1] + lane - w)
            slot = jnp.where(slot >= jnp.where(left, wants[0], wants[1]), slot, -2).astype(F32)
            tok_pos = jnp.where(left, pos_t[:, e:e + 1], pos_t[:, e + 1:e + 2])
            hots.append(jnp.where(tok_pos == slot, 1.0, 0.0).astype(BF16))
        hot = jnp.concatenate(hots, axis=1)
        half = tt // 2
        return jnp.concatenate([_dot(hot[:half], win_ref[...]), _dot(hot[half:], win_ref[...])], axis=0)

    acc = lax.fori_loop(1, n_pass, lambda p, a: a + window_sum(p), window_sum(0))
    h = h_ref[0] + gate_ref[0] * acc
    if final:
        h = h * lax.rsqrt(jnp.mean(h * h, axis=-1, keepdims=True) + EPS) * fg_ref[...]
    o_ref[0] = h


def moe_combine(tile_start, pos, y, h, gate, final_g, *, tt, final):
    b, n, d = h.shape
    e, _, cap, _ = y.shape
    grid_spec = pltpu.PrefetchScalarGridSpec(
        num_scalar_prefetch=1,
        grid=(b, n // tt),
        in_specs=[pl.BlockSpec((1, e, tt), lambda bi, i, s: (bi, 0, i)),
                  pl.BlockSpec((e, 1, cap, d), lambda bi, i, s: (0, bi, 0, 0), pipeline_mode=pl.Buffered(1)),
                  pl.BlockSpec((1, tt, d), lambda bi, i, s: (bi, i, 0)),
                  pl.BlockSpec((1, 1, d), lambda bi, i, s: (bi, 0, 0)),
                  pl.BlockSpec((1, d), lambda bi, i, s: (0, 0))],
        out_specs=pl.BlockSpec((1, tt, d), lambda bi, i, s: (bi, i, 0)),
        scratch_shapes=[pltpu.VMEM((e * COMBINE_WINDOW, d), BF16)])
    return pl.pallas_call(
        functools.partial(_combine_body, tt=tt, final=final),
        grid_spec=grid_spec,
        out_shape=jax.ShapeDtypeStruct((b, n, d), F32),
        compiler_params=_params(("arbitrary", "arbitrary")),
        name="moe_combine",
    )(tile_start, pos, y, h, gate, final_g)


def _expert_choice_moe(h, g, shift, scale, gate2, router, w1, w3, w2, layer, final_g, *, final):
    b, n, d = h.shape
    e = router.shape[1]
    cap = CAPACITY_FACTOR * n // e
    a, aff_t = moe_router(h, g, shift, scale, router, tm=MATMUL_ROWS)
    pos, idx, gate, start = moe_select(aff_t, cap)
    rows = idx[:, :, 0, :] + (jnp.arange(b, dtype=jnp.int32) * n)[:, None, None]
    rows = rows.transpose(1, 0, 2).reshape(e, 1, b * cap)
    gate_e = gate[:, :, 0, :].transpose(1, 0, 2).reshape(e, b * cap, 1)
    y = moe_ffn(rows, a.reshape(b * n, d), gate_e, w1, w3, w2, layer, tf=FFN_TILE)
    stride = COMBINE_TOKENS // LANES
    tile_start = start[:, :, 0, 0:n // LANES + 1:stride].astype(jnp.int32)
    return moe_combine(tile_start, pos[:, :, 0, :], y.reshape(e, b, cap, d), h, gate2, final_g,
                       tt=COMBINE_TOKENS, final=final)


def kernel(x, c, ctx, c_ctx, mod_w, mod_b, norm1_g, norm2_g, router, w1, w3, w2, attn_w_in, attn_w_out,
           na_rpb, q_norm_g, k_norm_g, sgu_w_in, sgu_w_out, sgu_ws, sgu_b, sgu_ln_g, sgu_ln_b, final_norm_g):
    b, n, d = x.shape
    assert b + 1 <= SUBLANES and n <= (1 << (2 * TOKEN_SPLIT_BITS + 2))
    cvec = jnp.concatenate([c, c_ctx[None], jnp.zeros((SUBLANES - b - 1, d), c.dtype)], axis=0)
    mods = adaln(cvec, mod_w, mod_b)
    cos, sin = rope_tables(n)
    h = x
    for l in range(DEPTH):
        mod = [mods[l, :, k * d:(k + 1) * d] for k in range(N_MOD)]
        sh1, sc1, g1, sh2, sc2, g2 = (m[:b, None, :] for m in mod)
        n1 = norm1_g[l].reshape(1, d)
        if l % 2 == 0:
            ev = l // 2
            csh1, csc1 = (jnp.broadcast_to(m[b][None, None, :], (b, 1, d)) for m in mod[:2])
            w_in = attn_w_in[ev].astype(BF16)
            proj = norm_mod_matmul(h, n1, sh1, sc1, w_in)
            ctxp = norm_mod_matmul(ctx, n1, csh1, csc1, w_in[:, Q_COLS:])
            tl, tr = na_bias_tables(na_rpb[ev])
            heads_a = neighbourhood_attention(proj, ctxp, tl, tr)
            heads_b = gqa_attention(proj, ctxp, q_norm_g[ev].reshape(1, -1), k_norm_g[ev].reshape(1, -1),
                                    cos, sin, tq=GQA_Q_TILE)
            w_out = attn_w_out[ev].astype(BF16)
            h = matmul_residual([(heads_a, w_out[:NA_W]), (heads_b, w_out[NA_W:])], h, g1)
        else:
            o = l // 2
            z = norm_mod_matmul(h, n1, sh1, sc1, sgu_w_in[o].astype(BF16), gelu=True)
            mixed = sgu_mix(z, sgu_ws[o], sgu_b[o], sgu_ln_g[o], sgu_ln_b[o])
            h = matmul_residual([(mixed, sgu_w_out[o].astype(BF16))], h, g1)
        h = _expert_choice_moe(h, norm2_g[l].reshape(1, d), sh2, sc2, g2, router[l], w1, w3, w2, l,
                               final_norm_g.reshape(1, d), final=(l == DEPTH - 1))
    return h
```

```python
import functools

import jax
import jax.numpy as jnp
from jax import lax
from jax.experimental import pallas as pl
from jax.experimental.pallas import tpu as pltpu

D_MODEL = 2048
DEPTH = 2
GRID_W = 64
CTX_LEN = 256
HEAD_DIM = 128
ATTN_SCALE = HEAD_DIM ** -0.5
NA_HEADS = 8
NA_KH = 8
NA_KW = 16
GQA_Q_HEADS = 8
GQA_KV_HEADS = 2
GQA_GROUP = GQA_Q_HEADS // GQA_KV_HEADS
ROPE_THETA = 10000.0
CHUNK = 128
SGU_GROUPS = 8
SGU_WIDTH = 2 * D_MODEL
N_EXPERTS = 16
CAPACITY_FACTOR = 2
N_MOD = 6
EPS = 1e-6
NA_W = NA_HEADS * HEAD_DIM
Q_COLS = NA_W + GQA_Q_HEADS * HEAD_DIM

QA_BLK, QB_BLK, KA_BLK, VA_BLK, KB_BLK, VB_BLK = 0, 8, 16, 24, 32, 34
CKA_BLK, CVA_BLK, CKB_BLK, CVB_BLK = 0, 8, 16, 18

NEG_INF = -1e30
V7X_VMEM_LIMIT_BYTES = 56 * 1024 * 1024
BF16 = jnp.bfloat16
F32 = jnp.float32

NA_Q_ROWS = 4
NA_K_ROWS = 12
NA_MASKED = 2 * NA_KH - 1
LOG2_E = 1.4426950408889634
NA_HEADS_PER_STEP = 4

GQA_KEY_CHUNK = 512
GQA_Q_CHUNK = 256
GQA_Q_TILE = 2048
LANES = 128
SUBLANES = 8
F32_MAGNITUDE_BITS = 31
TOKEN_SPLIT_BITS = 6
BF16_ROWS = 16
FFN_TILE = 256
GATHER_CAST_ROWS = 256
COMBINE_TOKENS = 256
COMBINE_WINDOW = 64
SGU_CHUNKS_PER_STEP = 4
MATMUL_ROWS = 1024
MATMUL_MAX_COLS = 1536
RESIDUAL_COLS = (512, 1024)
RESIDUAL_VMEM_BUDGET = 40 * 1024 * 1024
MXU_COLS = 256


def _col_tile(n_out):
    return max(t for t in range(MXU_COLS, MATMUL_MAX_COLS + 1, MXU_COLS) if n_out % t == 0)


def _params(semantics):
    return pltpu.CompilerParams(dimension_semantics=semantics,
                                vmem_limit_bytes=V7X_VMEM_LIMIT_BYTES)


def _dot(a, b):
    return jnp.dot(a, b, preferred_element_type=F32)


def _dot_nt(a, b):
    return lax.dot_general(a, b, (((1,), (1,)), ((), ())), preferred_element_type=F32)


def _adaln_body(c_ref, w_ref, b_ref, o_ref):
    c = c_ref[...]
    s = (c * jax.nn.sigmoid(c)).astype(BF16)
    o_ref[0] = _dot(s, w_ref[0].astype(BF16)) + b_ref[0]


def adaln(cvec, mod_w, mod_b, tn=1024):
    L, d, n = mod_w.shape
    return pl.pallas_call(
        _adaln_body,
        grid=(L, n // tn),
        in_specs=[pl.BlockSpec((SUBLANES, d), lambda l, j: (0, 0)),
                  pl.BlockSpec((1, d, tn), lambda l, j: (l, 0, j)),
                  pl.BlockSpec((1, 1, tn), lambda l, j: (l, 0, j))],
        out_specs=pl.BlockSpec((1, SUBLANES, tn), lambda l, j: (l, 0, j)),
        out_shape=jax.ShapeDtypeStruct((L, SUBLANES, n), F32),
        compiler_params=_params(("parallel", "arbitrary")),
        name="adaln",
    )(cvec, mod_w, mod_b.reshape(L, 1, n))


def _modulated(x, g, sh, sc):
    y = x * lax.rsqrt(jnp.mean(x * x, axis=-1, keepdims=True) + EPS) * g
    return y * (1 + sc) + sh


def _nmm_body(x_ref, g_ref, sh_ref, sc_ref, w_ref, o_ref, a_ref, *, gelu):
    @pl.when(pl.program_id(2) == 0)
    def _():
        a_ref[...] = _modulated(x_ref[0], g_ref[...], sh_ref[0], sc_ref[0]).astype(BF16)

    z = _dot(a_ref[...], w_ref[...])
    if gelu:
        z = jax.nn.gelu(z)
    o_ref[0] = z.astype(o_ref.dtype)


def norm_mod_matmul(h, g, shift, scale, w, *, gelu=False):
    b, n, d = h.shape
    n_out = w.shape[1]
    tm = min(n, MATMUL_ROWS)
    tn = _col_tile(n_out)
    return pl.pallas_call(
        functools.partial(_nmm_body, gelu=gelu),
        grid=(b, n // tm, n_out // tn),
        in_specs=[pl.BlockSpec((1, tm, d), lambda bi, i, j: (bi, i, 0)),
                  pl.BlockSpec((1, d), lambda bi, i, j: (0, 0)),
                  pl.BlockSpec((1, 1, d), lambda bi, i, j: (bi, 0, 0)),
                  pl.BlockSpec((1, 1, d), lambda bi, i, j: (bi, 0, 0)),
                  pl.BlockSpec((d, tn), lambda bi, i, j: (0, j))],
        out_specs=pl.BlockSpec((1, tm, tn), lambda bi, i, j: (bi, i, j)),
        out_shape=jax.ShapeDtypeStruct((b, n, n_out), BF16),
        scratch_shapes=[pltpu.VMEM((tm, d), BF16)],
        compiler_params=_params(("parallel", "parallel", "arbitrary")),
        name="norm_mod_matmul",
    )(h, g, shift, scale, w)


def _mmr_body(*refs, n_parts):
    a_refs, w_refs = refs[:n_parts], refs[n_parts:2 * n_parts]
    h_ref, gate_ref, o_ref = refs[2 * n_parts:]
    acc = _dot(a_refs[0][0], w_refs[0][...])
    for a_ref, w_ref in zip(a_refs[1:], w_refs[1:]):
        acc = acc + _dot(a_ref[0], w_ref[...])
    o_ref[0] = h_ref[0] + gate_ref[0] * acc


def matmul_residual(parts, h, gate):
    b, n, d = h.shape
    tm = min(n, MATMUL_ROWS)
    k_total = sum(w.shape[0] for _, w in parts)

    def step_bytes(tn):
        return 2 * (2 * tm * k_total + 2 * k_total * tn + 4 * tm * tn + 4 * tm * tn)

    tn = max(t for t in RESIDUAL_COLS if d % t == 0 and step_bytes(t) <= RESIDUAL_VMEM_BUDGET)
    a_list = [a for a, _ in parts]
    w_list = [w for _, w in parts]
    return pl.pallas_call(
        functools.partial(_mmr_body, n_parts=len(parts)),
        grid=(b, n // tm, d // tn),
        in_specs=([pl.BlockSpec((1, tm, a.shape[2]), lambda bi, i, j: (bi, i, 0)) for a in a_list]
                  + [pl.BlockSpec((w.shape[0], tn), lambda bi, i, j: (0, j)) for w in w_list]
                  + [pl.BlockSpec((1, tm, tn), lambda bi, i, j: (bi, i, j)),
                     pl.BlockSpec((1, 1, tn), lambda bi, i, j: (bi, 0, j))]),
        out_specs=pl.BlockSpec((1, tm, tn), lambda bi, i, j: (bi, i, j)),
        out_shape=jax.ShapeDtypeStruct((b, n, d), F32),
        compiler_params=_params(("parallel", "parallel", "arbitrary")),
        name="matmul_residual",
    )(*a_list, *w_list, h, gate)


def na_bias_tables(rpb):
    cols = jnp.arange(GRID_W)
    start = jnp.clip(cols - NA_KW // 2, 0, GRID_W - NA_KW)
    j = jnp.arange(GRID_W)
    inside = (j[None, :] >= start[:, None]) & (j[None, :] < start[:, None] + NA_KW)
    dcol = jnp.clip(j[None, :] - cols[:, None] + NA_KW - 1, 0, 2 * NA_KW - 2)
    t = jnp.where(inside[None, None], rpb[:, :, dcol] * LOG2_E, NEG_INF)
    t = jnp.concatenate([t, jnp.full_like(t[:, :1], NEG_INF)], axis=1)
    z = jnp.zeros_like(t)
    return jnp.concatenate([t, z], axis=-1), jnp.concatenate([z, t], axis=-1)


def _na_body(q_ref, k_ref, v_ref, kc_ref, vc_ref, tl_ref, tr_ref, o_ref, s_ref):
    rows = k_ref.shape[1] // GRID_W
    r0 = pl.program_id(2) * NA_Q_ROWS
    k0 = jnp.clip(r0 - NA_KH // 2, 0, rows - NA_K_ROWS)
    kstart = pl.multiple_of(k0 * GRID_W, GRID_W)
    keys = pl.ds(kstart, NA_K_ROWS * GRID_W)
    scale = ATTN_SCALE * LOG2_E

    def table_index(kr, r, rs):
        valid = (kr >= rs) & (kr < rs + NA_KH)
        return jnp.where(valid, kr - r + NA_KH - 1, NA_MASKED)

    table_ids = []
    for qr in range(NA_Q_ROWS):
        r = r0 + qr
        rs = jnp.clip(r - NA_KH // 2, 0, rows - NA_KH)
        table_ids.append([(table_index(k0 + 2 * p, r, rs), table_index(k0 + 2 * p + 1, r, rs))
                          for p in range(NA_K_ROWS // 2)])

    for hh in range(NA_HEADS_PER_STEP):
        lanes = slice(hh * HEAD_DIM, (hh + 1) * HEAD_DIM)
        q = q_ref[0, :, lanes]
        s_ref[hh] = _dot_nt(q, k_ref[0, keys, lanes]) * scale
        s_ctx = _dot_nt(q, kc_ref[0, :, lanes]) * scale
        for qr in range(NA_Q_ROWS):
            for p in range(NA_K_ROWS // 2):
                ia, ib = table_ids[qr][p]
                blk = (hh, slice(qr * GRID_W, (qr + 1) * GRID_W), slice(p * 2 * GRID_W, (p + 1) * 2 * GRID_W))
                s_ref[blk] = s_ref[blk] + tl_ref[hh, ia] + tr_ref[hh, ib]
        s_loc = s_ref[hh]
        m = jnp.maximum(jnp.max(s_loc, axis=-1, keepdims=True), jnp.max(s_ctx, axis=-1, keepdims=True))
        p_loc = jnp.exp2(s_loc - m)
        p_ctx = jnp.exp2(s_ctx - m)
        denom = jnp.sum(p_loc, axis=-1, keepdims=True) + jnp.sum(p_ctx, axis=-1, keepdims=True)
        o = _dot(p_loc.astype(BF16), v_ref[0, keys, lanes]) + _dot(p_ctx.astype(BF16), vc_ref[0, :, lanes])
        o_ref[0, :, lanes] = (o / denom).astype(o_ref.dtype)


def neighbourhood_attention(proj, ctxp, tl, tr):
    b, n, _ = proj.shape
    lc = ctxp.shape[1]
    tq = NA_Q_ROWS * GRID_W
    hs = NA_HEADS_PER_STEP
    w = hs * HEAD_DIM
    table = pl.BlockSpec((hs, 2 * NA_KH, GRID_W, 2 * GRID_W), lambda bi, h, t: (h, 0, 0, 0))
    return pl.pallas_call(
        _na_body,
        grid=(b, NA_HEADS // hs, n // tq),
        in_specs=[pl.BlockSpec((1, tq, w), lambda bi, h, t: (bi, t, QA_BLK // hs + h)),
                  pl.BlockSpec((1, n, w), lambda bi, h, t: (bi, 0, KA_BLK // hs + h)),
                  pl.BlockSpec((1, n, w), lambda bi, h, t: (bi, 0, VA_BLK // hs + h)),
                  pl.BlockSpec((1, lc, w), lambda bi, h, t: (bi, 0, CKA_BLK // hs + h)),
                  pl.BlockSpec((1, lc, w), lambda bi, h, t: (bi, 0, CVA_BLK // hs + h)),
                  table, table],
        out_specs=pl.BlockSpec((1, tq, w), lambda bi, h, t: (bi, t, h)),
        out_shape=jax.ShapeDtypeStruct((b, n, NA_W), BF16),
        scratch_shapes=[pltpu.VMEM((hs, tq, NA_K_ROWS * GRID_W), F32)],
        compiler_params=_params(("parallel", "parallel", "arbitrary")),
        name="neighbourhood_attention",
    )(proj, proj, proj, ctxp, ctxp, tl, tr)


def rope_tables(n):
    t = jnp.arange(n)
    row = (t // GRID_W).astype(F32)
    col = (t % GRID_W).astype(F32)
    axis_dims = HEAD_DIM // 2
    inv_freq = ROPE_THETA ** (-jnp.arange(0, axis_dims, 2, dtype=F32) / axis_dims)
    ang_r = row[:, None] * inv_freq
    ang_c = col[:, None] * inv_freq
    cos = jnp.concatenate([jnp.cos(ang_r), jnp.cos(ang_r), jnp.cos(ang_c), jnp.cos(ang_c)], axis=-1)
    sin = jnp.concatenate([-jnp.sin(ang_r), jnp.sin(ang_r), -jnp.sin(ang_c), jnp.sin(ang_c)], axis=-1)
    return cos, sin


def _head_rms(x, g):
    return x * lax.rsqrt(jnp.mean(x * x, axis=-1, keepdims=True) + EPS) * g


def _rope(x, cos, sin):
    quarter = HEAD_DIM // 4
    lane = lax.broadcasted_iota(jnp.int32, x.shape, 1)
    partner = jnp.where(lane % (2 * quarter) < quarter,
                        pltpu.roll(x, HEAD_DIM - quarter, 1), pltpu.roll(x, quarter, 1))
    return x * cos + partner * sin


def _gqa_body(q_ref, k_ref, v_ref, kc_ref, vc_ref, qg_ref, kg_ref, cq_ref, sq_ref, ck_ref, sk_ref,
              o_ref, kt_ref):
    n = k_ref.shape[1]
    lc = kc_ref.shape[1]

    @pl.when(pl.program_id(2) == 0)
    def _():
        kg = kg_ref[...]
        for c in range(n // GQA_KEY_CHUNK):
            rows = slice(c * GQA_KEY_CHUNK, (c + 1) * GQA_KEY_CHUNK)
            kn = _rope(_head_rms(k_ref[0, rows, :].astype(F32), kg), ck_ref[rows, :], sk_ref[rows, :])
            kt_ref[:, rows] = kn.T.astype(BF16)
        kt_ref[:, n:n + lc] = _head_rms(kc_ref[0].astype(F32), kg).T.astype(BF16)

    for c in range(q_ref.shape[1] // GQA_Q_CHUNK):
        rows = slice(c * GQA_Q_CHUNK, (c + 1) * GQA_Q_CHUNK)
        q = _rope(_head_rms(q_ref[0, rows, :].astype(F32), qg_ref[...]), cq_ref[rows, :], sq_ref[rows, :])
        q = (q * (ATTN_SCALE * LOG2_E)).astype(BF16)
        s = _dot(q, kt_ref[...])
        p = jnp.exp2(s - jnp.max(s, axis=-1, keepdims=True))
        denom = jnp.sum(p, axis=-1, keepdims=True)
        pb = p.astype(BF16)
        o = _dot(pb[:, :n], v_ref[0]) + _dot(pb[:, n:], vc_ref[0])
        o_ref[0, rows, :] = (o / denom).astype(o_ref.dtype)


def gqa_attention(proj, ctxp, qn_g, kn_g, cos, sin, *, tq):
    b, n, _ = proj.shape
    lc = ctxp.shape[1]
    hd = HEAD_DIM
    nq = n // tq

    def qhead(kv, i):
        return kv * GQA_GROUP + i // nq

    return pl.pallas_call(
        _gqa_body,
        grid=(b, GQA_KV_HEADS, GQA_GROUP * nq),
        in_specs=[pl.BlockSpec((1, tq, hd), lambda bi, kv, i: (bi, i % nq, QB_BLK + qhead(kv, i))),
                  pl.BlockSpec((1, n, hd), lambda bi, kv, i: (bi, 0, KB_BLK + kv)),
                  pl.BlockSpec((1, n, hd), lambda bi, kv, i: (bi, 0, VB_BLK + kv)),
                  pl.BlockSpec((1, lc, hd), lambda bi, kv, i: (bi, 0, CKB_BLK + kv)),
                  pl.BlockSpec((1, lc, hd), lambda bi, kv, i: (bi, 0, CVB_BLK + kv)),
                  pl.BlockSpec((1, hd), lambda bi, kv, i: (0, 0)),
                  pl.BlockSpec((1, hd), lambda bi, kv, i: (0, 0)),
                  pl.BlockSpec((tq, hd), lambda bi, kv, i: (i % nq, 0)),
                  pl.BlockSpec((tq, hd), lambda bi, kv, i: (i % nq, 0)),
                  pl.BlockSpec((n, hd), lambda bi, kv, i: (0, 0)),
                  pl.BlockSpec((n, hd), lambda bi, kv, i: (0, 0))],
        out_specs=pl.BlockSpec((1, tq, hd), lambda bi, kv, i: (bi, i % nq, qhead(kv, i))),
        out_shape=jax.ShapeDtypeStruct((b, n, GQA_Q_HEADS * hd), BF16),
        scratch_shapes=[pltpu.VMEM((hd, n + lc), BF16)],
        compiler_params=_params(("parallel", "parallel", "arbitrary")),
        name="gqa_attention",
    )(proj, proj, proj, ctxp, ctxp, qn_g, kn_g, cos, sin, cos, sin)


def _sgu_body(z_ref, ws_ref, bs_ref, g_ref, b_ref, o_ref):
    dg = SGU_WIDTH // SGU_GROUPS
    for c in range(z_ref.shape[1] // CHUNK):
        rows = slice(c * CHUNK, (c + 1) * CHUNK)
        v = z_ref[0, rows, SGU_WIDTH:].astype(F32)
        mu = jnp.mean(v, axis=-1, keepdims=True)
        vc = v - mu
        var = jnp.mean(vc * vc, axis=-1, keepdims=True)
        vn = (vc * lax.rsqrt(var + EPS) * g_ref[...] + b_ref[...]).astype(BF16)
        for g in range(SGU_GROUPS):
            cols = slice(g * dg, (g + 1) * dg)
            mixed = _dot(ws_ref[g].astype(BF16), vn[:, cols]) + bs_ref[:, g:g + 1]
            o_ref[0, rows, cols] = (z_ref[0, rows, cols].astype(F32) * mixed).astype(o_ref.dtype)


def sgu_mix(z, ws, bs, ln_g, ln_b):
    b, n, _ = z.shape
    rows = SGU_CHUNKS_PER_STEP * CHUNK
    return pl.pallas_call(
        _sgu_body,
        grid=(b, n // rows),
        in_specs=[pl.BlockSpec((1, rows, 2 * SGU_WIDTH), lambda bi, c: (bi, c, 0)),
                  pl.BlockSpec((SGU_GROUPS, CHUNK, CHUNK), lambda bi, c: (0, 0, 0)),
                  pl.BlockSpec((CHUNK, SGU_GROUPS), lambda bi, c: (0, 0)),
                  pl.BlockSpec((1, SGU_WIDTH), lambda bi, c: (0, 0)),
                  pl.BlockSpec((1, SGU_WIDTH), lambda bi, c: (0, 0))],
        out_specs=pl.BlockSpec((1, rows, SGU_WIDTH), lambda bi, c: (bi, c, 0)),
        out_shape=jax.ShapeDtypeStruct((b, n, SGU_WIDTH), BF16),
        compiler_params=_params(("parallel", "arbitrary")),
        name="sgu_mix",
    )(z, ws, bs.T, ln_g.reshape(1, -1), ln_b.reshape(1, -1))


def _split_bf16(x):
    hi = x.astype(BF16)
    return hi, (x - hi.astype(F32)).astype(BF16)


def _router_body(x_ref, g_ref, sh_ref, sc_ref, rt_ref, a_ref, aff_ref):
    a = _modulated(x_ref[0], g_ref[...], sh_ref[0], sc_ref[0])
    a_ref[0] = a
    a_hi, a_lo = _split_bf16(a)
    r_hi, r_lo = _split_bf16(rt_ref[...])
    logits = _dot_nt(r_hi, a_hi) + (_dot_nt(r_hi, a_lo) + _dot_nt(r_lo, a_hi))
    e = jnp.exp(logits - jnp.max(logits, axis=0, keepdims=True))
    aff_ref[0] = e / jnp.sum(e, axis=0, keepdims=True)


def moe_router(h, g, shift, scale, router, *, tm):
    b, n, d = h.shape
    e = router.shape[1]
    return pl.pallas_call(
        _router_body,
        grid=(b, n // tm),
        in_specs=[pl.BlockSpec((1, tm, d), lambda bi, i: (bi, i, 0)),
                  pl.BlockSpec((1, d), lambda bi, i: (0, 0)),
                  pl.BlockSpec((1, 1, d), lambda bi, i: (bi, 0, 0)),
                  pl.BlockSpec((1, 1, d), lambda bi, i: (bi, 0, 0)),
                  pl.BlockSpec((e, d), lambda bi, i: (0, 0))],
        out_specs=[pl.BlockSpec((1, tm, d), lambda bi, i: (bi, i, 0)),
                   pl.BlockSpec((1, e, tm), lambda bi, i: (bi, 0, i))],
        out_shape=[jax.ShapeDtypeStruct((b, n, d), F32), jax.ShapeDtypeStruct((b, e, n), F32)],
        compiler_params=_params(("parallel", "arbitrary")),
        name="moe_router",
    )(h, g, shift, scale, router.T)


def _row_copy(a_hbm, xf_ref, sem, src_row, dst_row):
    return pltpu.make_async_copy(a_hbm.at[pl.ds(src_row, 1), :], xf_ref.at[pl.ds(dst_row, 1), :], sem.at[0])


def _ffn_body(rows_ref, next_rows_ref, a_hbm, w1_ref, w3_ref, w2_ref, gate_ref, o_ref, xf_ref, xb_ref, hid_ref,
              sem, *, nt, tf):
    e = pl.program_id(0)
    j = pl.program_id(1)
    n_steps = pl.num_programs(1)
    m = xf_ref.shape[0]
    rows_per_step = m // (nt + xf_ref.shape[1] // tf)

    def wait_all_rows():
        def wait(r, carry):
            _row_copy(a_hbm, xf_ref, sem, 0, r).wait()
            return carry

        lax.fori_loop(0, m, wait, 0, unroll=16)

    def prefetch_next_rows():
        first = j * rows_per_step
        for r in range(rows_per_step):
            _row_copy(a_hbm, xf_ref, sem, next_rows_ref[0, 0, first + r], first + r).start(priority=r % 2)

    @pl.when((e == 0) & (j == 0))
    def _():
        def start(k, carry):
            for p in range(2):
                r = 2 * k + p
                _row_copy(a_hbm, xf_ref, sem, rows_ref[0, 0, r], r).start(priority=p)
            return carry

        lax.fori_loop(0, m // 2, start, 0, unroll=4)

    @pl.when(j == 0)
    def _():
        wait_all_rows()

        def cast(i, carry):
            sl = pl.ds(pl.multiple_of(i * GATHER_CAST_ROWS, GATHER_CAST_ROWS), GATHER_CAST_ROWS)
            xb_ref[sl, :] = xf_ref[sl, :].astype(BF16)
            return carry

        lax.fori_loop(0, m // GATHER_CAST_ROWS, cast, 0)

    @pl.when(j < nt)
    def _():
        prefetch_next_rows()
        x = xb_ref[...]
        h1 = _dot(x, w1_ref[0, 0].astype(BF16))
        h3 = _dot(x, w3_ref[0, 0].astype(BF16))
        hid_ref[j] = (h1 * jax.nn.sigmoid(h1) * h3).astype(BF16)

    @pl.when(j >= nt)
    def _():
        prefetch_next_rows()
        acc = _dot(hid_ref[0], w2_ref[0, 0, 0:tf, :].astype(BF16))
        for k in range(1, nt):
            acc = acc + _dot(hid_ref[k], w2_ref[0, 0, k * tf:(k + 1) * tf, :].astype(BF16))
        o_ref[0] = (acc * gate_ref[0]).astype(o_ref.dtype)

    @pl.when((e == pl.num_programs(0) - 1) & (j == n_steps - 1))
    def _():
        wait_all_rows()


def moe_ffn(rows, a, gate, w1, w3, w2, layer, *, tf):
    e, _, m = rows.shape
    d = a.shape[1]
    f = w1.shape[3]
    nt = f // tf
    nd = d // tf
    assert m % (nt + nd) == 0
    return pl.pallas_call(
        functools.partial(_ffn_body, nt=nt, tf=tf),
        grid=(e, nt + nd),
        in_specs=[pl.BlockSpec((1, 1, m), lambda ei, j: (ei, 0, 0), memory_space=pltpu.SMEM),
                  pl.BlockSpec((1, 1, m), lambda ei, j: ((ei + 1) % e, 0, 0), memory_space=pltpu.SMEM),
                  pl.BlockSpec(memory_space=pl.ANY),
                  pl.BlockSpec((1, 1, d, tf), lambda ei, j: (layer, ei, 0, jnp.minimum(j, nt - 1))),
                  pl.BlockSpec((1, 1, d, tf), lambda ei, j: (layer, ei, 0, jnp.minimum(j, nt - 1))),
                  pl.BlockSpec((1, 1, f, tf), lambda ei, j: (layer, ei, 0, jnp.maximum(j - nt, 0))),
                  pl.BlockSpec((1, m, 1), lambda ei, j: (ei, 0, 0))],
        out_specs=pl.BlockSpec((1, m, tf), lambda ei, j: (ei, 0, jnp.maximum(j - nt, 0))),
        out_shape=jax.ShapeDtypeStruct((e, m, d), BF16),
        scratch_shapes=[pltpu.VMEM((m, d), F32), pltpu.VMEM((m, d), BF16), pltpu.VMEM((nt, m, tf), BF16),
                        pltpu.SemaphoreType.DMA((1,))],
        compiler_params=_params(("arbitrary", "arbitrary")),
        name="moe_ffn",
    )(rows, rows, a, w1, w3, w2, gate)


def prefix_constants(n):
    i = jnp.arange(LANES)
    within = (i[:, None] < i[None, :]).astype(BF16)
    tok_tile = jnp.arange(n) // LANES
    before = (tok_tile[:, None] < i[None, :]).astype(BF16)
    return within, before


def _exclusive_prefix(mask, within_ref, before_ref):
    n = mask.shape[1]
    m = mask.astype(F32).astype(BF16)
    tile_start = _dot(m, before_ref[...])
    parts = []
    for t in range(n // LANES):
        cols = slice(t * LANES, (t + 1) * LANES)
        parts.append(_dot(m[:, cols], within_ref[...]) + tile_start[:, t:t + 1])
    return jnp.concatenate(parts, axis=1), tile_start


def _select_body(aff_ref, within_ref, before_ref, pos_ref, idx_ref, gate_ref, start_ref, pos_all, start_all,
                 *, cap):
    e = pl.program_id(1)
    n = aff_ref.shape[2]

    @pl.when(e == 0)
    def _():
        aff = aff_ref[0]

        def refine(i, thr):
            cand = thr | lax.shift_left(jnp.int32(1), F32_MAGNITUDE_BITS - 1 - i)
            cnt = jnp.sum((aff >= pltpu.bitcast(cand, F32)).astype(F32), axis=-1, keepdims=True)
            return jnp.where(cnt >= cap, cand, thr)

        thr = lax.fori_loop(0, F32_MAGNITUDE_BITS, refine, jnp.zeros((aff.shape[0], 1), jnp.int32))
        above = aff >= pltpu.bitcast(thr + 1, F32)
        tied = (aff >= pltpu.bitcast(thr, F32)) & jnp.logical_not(above)
        need = cap - jnp.sum(above.astype(F32), axis=-1, keepdims=True)
        tied_rank, _ = _exclusive_prefix(tied, within_ref, before_ref)
        sel = above | (tied & (tied_rank < need))
        pos, tile_start = _exclusive_prefix(sel, within_ref, before_ref)
        pos_all[...] = jnp.where(sel, pos, -1.0)
        start_all[...] = tile_start

    a = aff_ref[0, pl.ds(e, 1), :]
    pos = pos_all[pl.ds(e, 1), :]
    pos_ref[0, 0] = pos
    start_ref[0, 0] = start_all[pl.ds(e, 1), :]

    slot = lax.broadcasted_iota(jnp.int32, (cap, n), 0).astype(F32)
    onehot = jnp.where(jnp.broadcast_to(pos, (cap, n)) == slot, 1.0, 0.0).astype(BF16)
    tok = lax.broadcasted_iota(jnp.int32, (SUBLANES, n), 1)
    row = lax.broadcasted_iota(jnp.int32, (SUBLANES, n), 0)
    a8 = jnp.broadcast_to(a, (SUBLANES, n))
    a_hi = a8.astype(BF16).astype(F32)
    a_mid = (a8 - a_hi).astype(BF16).astype(F32)
    a_lo = a8 - a_hi - a_mid
    table = jnp.where(row == 0, lax.shift_right_logical(tok, TOKEN_SPLIT_BITS).astype(F32),
                      jnp.where(row == 1, (tok & ((1 << TOKEN_SPLIT_BITS) - 1)).astype(F32),
                                jnp.where(row == 2, a_hi,
                                          jnp.where(row == 3, a_mid, jnp.where(row == 4, a_lo, 0.0)))))
    picked = _dot_nt(table.astype(BF16), onehot)
    idx_ref[0, 0] = (picked[0:1] * float(1 << TOKEN_SPLIT_BITS) + picked[1:2]).astype(jnp.int32)
    gate_ref[0, 0] = picked[2:3] + picked[3:4] + picked[4:5]


def moe_select(aff_t, cap):
    b, e, n = aff_t.shape
    within, before = prefix_constants(n)
    row = lambda last: pl.BlockSpec((1, 1, 1, last), lambda bi, ei: (bi, ei, 0, 0))
    return pl.pallas_call(
        functools.partial(_select_body, cap=cap),
        grid=(b, e),
        in_specs=[pl.BlockSpec((1, e, n), lambda bi, ei: (bi, 0, 0)),
                  pl.BlockSpec((LANES, LANES), lambda bi, ei: (0, 0)),
                  pl.BlockSpec((n, LANES), lambda bi, ei: (0, 0))],
        out_specs=[row(n), row(cap), row(cap), row(LANES)],
        out_shape=[jax.ShapeDtypeStruct((b, e, 1, n), F32), jax.ShapeDtypeStruct((b, e, 1, cap), jnp.int32),
                   jax.ShapeDtypeStruct((b, e, 1, cap), F32), jax.ShapeDtypeStruct((b, e, 1, LANES), F32)],
        scratch_shapes=[pltpu.VMEM((e, n), F32), pltpu.VMEM((e, LANES), F32)],
        compiler_params=_params(("parallel", "arbitrary")),
        name="moe_select",
    )(aff_t, within, before)


def _combine_body(start_ref, pos_ref, y_ref, h_ref, gate_ref, fg_ref, o_ref, win_ref, *, tt, final):
    bi = pl.program_id(0)
    ti = pl.program_id(1)
    n_exp, _, cap, d = y_ref.shape
    w = COMBINE_WINDOW

    los = [start_ref[bi, e, ti] for e in range(n_exp)]
    his = [start_ref[bi, e, ti + 1] for e in range(n_exp)]
    bases = [(lo // BF16_ROWS) * BF16_ROWS for lo in los]
    n_pass = functools.reduce(jnp.maximum, [(hi - base + w - 1) // w for hi, base in zip(his, bases)])
    lane = lax.broadcasted_iota(jnp.int32, (1, 2 * w), 1)
    left = lane < w
    pos_t = pos_ref[0].T

    def window_sum(p):
        hots = []
        for e in range(0, n_exp, 2):
            wants, starts = [], []
            for k in range(2):
                want = bases[e + k] + p * w
                st = pl.multiple_of(jnp.minimum(want, cap - w), BF16_ROWS)
                win_ref[(e + k) * w:(e + k + 1) * w, :] = y_ref[e + k, 0, pl.ds(st, w), :]
                wants.append(want)
                starts.append(st)
            slot = jnp.where(left, starts[0] + lane, starts[1] + lane - w)
            slot = jnp.where(slot >= jnp.where(left, wants[0], wants[1]), slot, -2).astype(F32)
            tok_pos = jnp.where(left, pos_t[:, e:e + 1], pos_t[:, e + 1:e + 2])
            hots.append(jnp.where(tok_pos == slot, 1.0, 0.0).astype(BF16))
        hot = jnp.concatenate(hots, axis=1)
        half = tt // 2
        return jnp.concatenate([_dot(hot[:half], win_ref[...]), _dot(hot[half:], win_ref[...])], axis=0)

    acc = lax.fori_loop(1, n_pass, lambda p, a: a + window_sum(p), window_sum(0))
    h = h_ref[0] + gate_ref[0] * acc
    if final:
        h = h * lax.rsqrt(jnp.mean(h * h, axis=-1, keepdims=True) + EPS) * fg_ref[...]
    o_ref[0] = h


def moe_combine(tile_start, pos, y, h, gate, final_g, *, tt, final):
    b, n, d = h.shape
    e, _, cap, _ = y.shape
    grid_spec = pltpu.PrefetchScalarGridSpec(
        num_scalar_prefetch=1,
        grid=(b, n // tt),
        in_specs=[pl.BlockSpec((1, e, tt), lambda bi, i, s: (bi, 0, i)),
                  pl.BlockSpec((e, 1, cap, d), lambda bi, i, s: (0, bi, 0, 0), pipeline_mode=pl.Buffered(1)),
                  pl.BlockSpec((1, tt, d), lambda bi, i, s: (bi, i, 0)),
                  pl.BlockSpec((1, 1, d), lambda bi, i, s: (bi, 0, 0)),
                  pl.BlockSpec((1, d), lambda bi, i, s: (0, 0))],
        out_specs=pl.BlockSpec((1, tt, d), lambda bi, i, s: (bi, i, 0)),
        scratch_shapes=[pltpu.VMEM((e * COMBINE_WINDOW, d), BF16)])
    return pl.pallas_call(
        functools.partial(_combine_body, tt=tt, final=final),
        grid_spec=grid_spec,
        out_shape=jax.ShapeDtypeStruct((b, n, d), F32),
        compiler_params=_params(("arbitrary", "arbitrary")),
        name="moe_combine",
    )(tile_start, pos, y, h, gate, final_g)


def _expert_choice_moe(h, g, shift, scale, gate2, router, w1, w3, w2, layer, final_g, *, final):
    b, n, d = h.shape
    e = router.shape[1]
    cap = CAPACITY_FACTOR * n // e
    a, aff_t = moe_router(h, g, shift, scale, router, tm=MATMUL_ROWS)
    pos, idx, gate, start = moe_select(aff_t, cap)
    rows = idx[:, :, 0, :] + (jnp.arange(b, dtype=jnp.int32) * n)[:, None, None]
    rows = rows.transpose(1, 0, 2).reshape(e, 1, b * cap)
    gate_e = gate[:, :, 0, :].transpose(1, 0, 2).reshape(e, b * cap, 1)
    y = moe_ffn(rows, a.reshape(b * n, d), gate_e, w1, w3, w2, layer, tf=FFN_TILE)
    stride = COMBINE_TOKENS // LANES
    tile_start = start[:, :, 0, 0:n // LANES + 1:stride].astype(jnp.int32)
    return moe_combine(tile_start, pos[:, :, 0, :], y.reshape(e, b, cap, d), h, gate2, final_g,
                       tt=COMBINE_TOKENS, final=final)


def kernel(x, c, ctx, c_ctx, mod_w, mod_b, norm1_g, norm2_g, router, w1, w3, w2, attn_w_in, attn_w_out,
           na_rpb, q_norm_g, k_norm_g, sgu_w_in, sgu_w_out, sgu_ws, sgu_b, sgu_ln_g, sgu_ln_b, final_norm_g):
    b, n, d = x.shape
    assert b + 1 <= SUBLANES and n <= (1 << (2 * TOKEN_SPLIT_BITS + 2))
    cvec = jnp.concatenate([c, c_ctx[None], jnp.zeros((SUBLANES - b - 1, d), c.dtype)], axis=0)
    mods = adaln(cvec, mod_w, mod_b)
    cos, sin = rope_tables(n)
    h = x
    for l in range(DEPTH):
        mod = [mods[l, :, k * d:(k + 1) * d] for k in range(N_MOD)]
        sh1, sc1, g1, sh2, sc2, g2 = (m[:b, None, :] for m in mod)
        n1 = norm1_g[l].reshape(1, d)
        if l % 2 == 0:
            ev = l // 2
            csh1, csc1 = (jnp.broadcast_to(m[b][None, None, :], (b, 1, d)) for m in mod[:2])
            w_in = attn_w_in[ev].astype(BF16)
            proj = norm_mod_matmul(h, n1, sh1, sc1, w_in)
            ctxp = norm_mod_matmul(ctx, n1, csh1, csc1, w_in[:, Q_COLS:])
            tl, tr = na_bias_tables(na_rpb[ev])
            heads_a = neighbourhood_attention(proj, ctxp, tl, tr)
            heads_b = gqa_attention(proj, ctxp, q_norm_g[ev].reshape(1, -1), k_norm_g[ev].reshape(1, -1),
                                    cos, sin, tq=GQA_Q_TILE)
            w_out = attn_w_out[ev].astype(BF16)
            h = matmul_residual([(heads_a, w_out[:NA_W]), (heads_b, w_out[NA_W:])], h, g1)
        else:
            o = l // 2
            z = norm_mod_matmul(h, n1, sh1, sc1, sgu_w_in[o].astype(BF16), gelu=True)
            mixed = sgu_mix(z, sgu_ws[o], sgu_b[o], sgu_ln_g[o], sgu_ln_b[o])
            h = matmul_residual([(mixed, sgu_w_out[o].astype(BF16))], h, g1)
        h = _expert_choice_moe(h, norm2_g[l].reshape(1, d), sh2, sc2, g2, router[l], w1, w3, w2, l,
                               final_norm_g.reshape(1, d), final=(l == DEPTH - 1))
    return h
```
